```python
import math
import jax, jax.numpy as jnp
from jax import lax
import numpy as np

D_MODEL = 1024
BATCH = 4
SEQ = 4096
DEPTH = 4

GRID_W = 64
CTX_LEN = 256
CHUNK = 64
RMS_EPS = 1e-6
MIX_W = D_MODEL
GLA_W = MIX_W // 4
GLA_HEAD_V = 64
GLA_HEADS = GLA_W // GLA_HEAD_V
GLA_HEAD_K = GLA_HEAD_V // 2
GLA_KDIM = GLA_HEADS * GLA_HEAD_K
GLA_RANK = 16
GLA_GATE_TAU = 16.0
SSD_W = MIX_W // 2
SSD_HEAD_DIM = 64
SSD_HEADS = SSD_W // SSD_HEAD_DIM
SSD_GROUPS = 2
SSD_STATE = 128
SSD_CONV_W = 5
SSD_CONV_CH = SSD_W + 2 * SSD_GROUPS * SSD_STATE
RET_W = MIX_W - GLA_W - SSD_W
RET_HEAD_DIM = 64
RET_HEADS = RET_W // RET_HEAD_DIM
ROPE_BASE = 10000.0
GLA_COLS = 2 * GLA_KDIM + 2 * GLA_W + 2 * GLA_RANK
SSD_COLS = SSD_W + SSD_CONV_CH + 2 * SSD_HEADS
RET_COLS = 4 * RET_W
IN_COLS = GLA_COLS + SSD_COLS + RET_COLS
FFN_HIDDEN = -(-8 * D_MODEL // (3 * 256)) * 256

kernel_name = "hybrid_gla_ssd_retention_dit_block"


def rms_norm(x, w, eps=RMS_EPS):
    xf = x.astype(jnp.float32)
    y = xf * lax.rsqrt(jnp.mean(xf * xf, axis=-1, keepdims=True) + eps)
    return (y * w.astype(jnp.float32)).astype(x.dtype)


def layer_norm(x, w, eps=RMS_EPS):
    xf = x.astype(jnp.float32)
    mu = jnp.mean(xf, axis=-1, keepdims=True)
    xc = xf - mu
    y = xc * lax.rsqrt(jnp.mean(xc * xc, axis=-1, keepdims=True) + eps)
    return (y * w.astype(jnp.float32)).astype(x.dtype)


def modulate(x, shift, scale):
    return x * (1 + scale) + shift


def depthwise_conv(u, w, b):
    pad = (w.shape[0] - 1) // 2
    y = lax.conv_general_dilated(u, w[:, None, :], window_strides=(1,), padding=[(pad, pad)],
                                 dimension_numbers=('NWC', 'WIO', 'NWC'),
                                 feature_group_count=u.shape[-1])
    return y + b


def chunked_scan(q, k, v, logg, s0):
    B, T, H, Dk = q.shape
    Dv = v.shape[-1]
    n = T // CHUNK
    scalar = logg.shape[-1] == 1
    mask = jnp.tril(jnp.ones((CHUNK, CHUNK), dtype=bool))[None, :, :, None, None]

    def split(a):
        return a.reshape(B, n, CHUNK, H, a.shape[-1]).swapaxes(0, 1)

    def step(S, inp):
        qc, kc, vc, gc = inp
        G = jnp.cumsum(gc, axis=1)
        diff = G[:, :, None] - G[:, None, :]
        dec = jnp.where(mask, jnp.exp(jnp.minimum(diff, 0.0)), 0.0)
        if scalar:
            scores = jnp.einsum('bihd,bjhd->bijh', qc, kc) * dec[..., 0]
        else:
            scores = jnp.einsum('bihd,bjhd,bijhd->bijh', qc, kc, dec)
        intra = jnp.einsum('bijh,bjhe->bihe', scores, vc)
        inter = jnp.einsum('bihd,bhde->bihe', qc * jnp.exp(G), S)
        G_last = G[:, -1:]
        S_new = jnp.exp(G_last[:, 0])[..., None] * S + jnp.einsum(
            'bjhd,bjhe->bhde', kc * jnp.exp(G_last - G), vc)
        return S_new, intra + inter

    S, o = lax.scan(step, s0, (split(q), split(k), split(v), split(logg)))
    o = o.swapaxes(0, 1).reshape(B, T, H, Dv)
    return o, S


def bidir_scan(ctx_in, lat_in):
    qc, kfc, kbc, vc, gfc, gbc = ctx_in
    ql, kfl, kbl, vl, gfl, gbl = lat_in
    B, _, H, Dk = qc.shape
    Dv = vc.shape[-1]
    s0 = jnp.zeros((B, H, Dk, Dv), qc.dtype)
    flip = lambda a: jnp.flip(a, axis=1)
    oc_f, S_f = chunked_scan(qc, kfc, vc, gfc, s0)
    ol_f, _ = chunked_scan(ql, kfl, vl, gfl, S_f)
    oc_b, S_b = chunked_scan(flip(qc), flip(kbc), flip(vc), flip(gbc), s0)
    ol_b, _ = chunked_scan(flip(ql), flip(kbl), flip(vl), flip(gbl), S_b)
    return oc_f + flip(oc_b), ol_f + flip(ol_b)


def gla_mixer(p_ctx, p_lat, gate_up, gate_b, norm_w):
    def prep(p):
        B, T, _ = p.shape
        q, k, v, r, lr = jnp.split(p, [GLA_KDIM, 2 * GLA_KDIM, 2 * GLA_KDIM + GLA_W,
                                       2 * GLA_KDIM + 2 * GLA_W], axis=-1)
        q = q.reshape(B, T, GLA_HEADS, GLA_HEAD_K) * GLA_HEAD_K ** -0.5
        k = k.reshape(B, T, GLA_HEADS, GLA_HEAD_K)
        v = v.reshape(B, T, GLA_HEADS, GLA_HEAD_V)
        z = jnp.einsum('btnr,nrk->btnk', lr.reshape(B, T, 2, GLA_RANK), gate_up) + gate_b
        logg = jax.nn.log_sigmoid(z) / GLA_GATE_TAU
        g_f = logg[:, :, 0].reshape(B, T, GLA_HEADS, GLA_HEAD_K)
        g_b = logg[:, :, 1].reshape(B, T, GLA_HEADS, GLA_HEAD_K)
        return (q, k, k, v, g_f, g_b), r

    in_c, r_c = prep(p_ctx)
    in_l, r_l = prep(p_lat)
    o_c, o_l = bidir_scan(in_c, in_l)

    def out(o, r):
        B, T = o.shape[:2]
        o = rms_norm(o, norm_w.reshape(GLA_HEADS, GLA_HEAD_V)).reshape(B, T, GLA_W)
        return o * jax.nn.silu(r)

    return out(o_c, r_c), out(o_l, r_l)


def ssd_mixer(p_ctx, p_lat, conv_w, conv_b, dt_bias, a_log, d_skip, norm_w):
    def prep(p):
        B, T, _ = p.shape
        z, xbc, dt = jnp.split(p, [SSD_W, SSD_W + SSD_CONV_CH], axis=-1)
        xbc = jax.nn.silu(depthwise_conv(xbc, conv_w, conv_b))
        xs, bm, cm = jnp.split(xbc, [SSD_W, SSD_W + SSD_GROUPS * SSD_STATE], axis=-1)
        rep = SSD_HEADS // SSD_GROUPS
        xs = xs.reshape(B, T, SSD_HEADS, SSD_HEAD_DIM)
        bm = jnp.repeat(bm.reshape(B, T, SSD_GROUPS, SSD_STATE), rep, axis=2)
        cm = jnp.repeat(cm.reshape(B, T, SSD_GROUPS, SSD_STATE), rep, axis=2)
        dt = jax.nn.softplus(dt.reshape(B, T, 2, SSD_HEADS) + dt_bias)
        logg = dt * (-jnp.exp(a_log))
        k_f = bm * dt[:, :, 0, :, None]
        k_b = bm * dt[:, :, 1, :, None]
        return (cm, k_f, k_b, xs, logg[:, :, 0, :, None], logg[:, :, 1, :, None]), z, xs

    in_c, z_c, x_c = prep(p_ctx)
    in_l, z_l, x_l = prep(p_lat)
    y_c, y_l = bidir_scan(in_c, in_l)

    def out(y, z, xs):
        B, T = y.shape[:2]
        y = (y + d_skip[:, None] * xs).reshape(B, T, SSD_W)
        return rms_norm(y * jax.nn.silu(z), norm_w)

    return out(y_c, z_c, x_c), out(y_l, z_l, x_l)


def apply_rope(t, cos, sin):
    half = t.shape[-1] // 2
    t1, t2 = t[..., :half], t[..., half:]
    return jnp.concatenate([t1 * cos - t2 * sin, t2 * cos + t1 * sin], axis=-1)


def retention_mixer(p_ctx, p_lat, cos, sin, norm_w):
    log_gamma = jnp.log1p(-jnp.exp2(-5.0 - jnp.arange(RET_HEADS, dtype=jnp.float32)))
    log_gamma = log_gamma.astype(p_lat.dtype)

    def prep(p, rotate):
        B, T, _ = p.shape
        q, k, v, g = jnp.split(p, 4, axis=-1)
        q = q.reshape(B, T, RET_HEADS, RET_HEAD_DIM) * RET_HEAD_DIM ** -0.5
        k = k.reshape(B, T, RET_HEADS, RET_HEAD_DIM)
        v = v.reshape(B, T, RET_HEADS, RET_HEAD_DIM)
        if rotate:
            q = apply_rope(q, cos, sin)
            k = apply_rope(k, cos, sin)
        lg = jnp.broadcast_to(log_gamma[:, None], (B, T, RET_HEADS, 1))
        return (q, k, k, v, lg, lg), g

    in_c, g_c = prep(p_ctx, False)
    in_l, g_l = prep(p_lat, True)
    o_c, o_l = bidir_scan(in_c, in_l)

    def out(o, g):
        B, T = o.shape[:2]
        o = layer_norm(o, norm_w.reshape(RET_HEADS, RET_HEAD_DIM)).reshape(B, T, RET_W)
        return o * jax.nn.silu(g)

    return out(o_c, g_c), out(o_l, g_l)


def swiglu(h, w13, w2):
    gate, up = jnp.split(h @ w13, 2, axis=-1)
    return (jax.nn.silu(gate) * up) @ w2


def setup_inputs(seed: int = 0) -> dict:
    key = jax.random.key(seed)
    ks = jax.random.split(key, 24)
    f32 = jnp.float32
    nrm = lambda k, shape, s: jax.random.normal(k, shape, f32) * s
    gain = lambda k, shape: 1.0 + 0.05 * jax.random.normal(k, shape, f32)
    D = D_MODEL
    dt = jnp.exp(jax.random.uniform(ks[17], (DEPTH, 2, SSD_HEADS), f32)
                 * (math.log(0.1) - math.log(0.001)) + math.log(0.001))
    return {
        "x": nrm(ks[0], (BATCH, SEQ, D), 1.0),
        "c": nrm(ks[1], (BATCH, D), 1.0),
        "ctx": nrm(ks[2], (BATCH, CTX_LEN, D), 1.0),
        "c_ctx": nrm(ks[3], (D,), 1.0),
        "ada_w": nrm(ks[4], (DEPTH, D, 6 * D), D ** -0.5),
        "ada_b": nrm(ks[5], (DEPTH, 6 * D), 0.02),
        "norm_mix_pre": gain(ks[6], (DEPTH, D)),
        "norm_mix_post": gain(ks[7], (DEPTH, D)),
        "norm_ffn_pre": gain(ks[8], (DEPTH, D)),
        "norm_ffn_post": gain(ks[9], (DEPTH, D)),
        "w_in": nrm(ks[10], (DEPTH, D, IN_COLS), D ** -0.5),
        "w_out": nrm(ks[11], (DEPTH, MIX_W, D), MIX_W ** -0.5),
        "gla_gate_up": nrm(ks[12], (DEPTH, 2, GLA_RANK, GLA_KDIM), GLA_RANK ** -0.5),
        "gla_gate_b": nrm(ks[13], (DEPTH, 2, GLA_KDIM), 0.1),
        "gla_norm": gain(ks[14], (DEPTH, GLA_W)),
        "ssd_conv_w": nrm(ks[15], (DEPTH, SSD_CONV_W, SSD_CONV_CH), SSD_CONV_W ** -0.5),
        "ssd_conv_b": nrm(ks[16], (DEPTH, SSD_CONV_CH), 0.02),
        "ssd_dt_bias": dt + jnp.log(-jnp.expm1(-dt)),
        "ssd_a_log": jnp.log(jax.random.uniform(ks[18], (DEPTH, 2, SSD_HEADS), f32, 1.0, 16.0)),
        "ssd_d": gain(ks[19], (DEPTH, SSD_HEADS)),
        "ssd_norm": gain(ks[20], (DEPTH, SSD_W)),
        "ret_norm": gain(ks[21], (DEPTH, RET_W)),
        "ffn_w13": nrm(ks[22], (DEPTH, D, 2 * FFN_HIDDEN), D ** -0.5),
        "ffn_w2": nrm(ks[23], (DEPTH, FFN_HIDDEN, D), FFN_HIDDEN ** -0.5),
    }


def reference(x, c, ctx, c_ctx, ada_w, ada_b, norm_mix_pre, norm_mix_post, norm_ffn_pre,
              norm_ffn_post, w_in, w_out, gla_gate_up, gla_gate_b, gla_norm, ssd_conv_w,
              ssd_conv_b, ssd_dt_bias, ssd_a_log, ssd_d, ssd_norm, ret_norm, ffn_w13, ffn_w2):
    T = x.shape[1]
    rows = T // GRID_W
    row = jnp.repeat(jnp.arange(rows), GRID_W).astype(jnp.float32)
    col = jnp.tile(jnp.arange(GRID_W), rows).astype(jnp.float32)
    n_freq = RET_HEAD_DIM // 4
    inv_freq = ROPE_BASE ** (-jnp.arange(n_freq, dtype=jnp.float32) / n_freq)
    ang = jnp.concatenate([row[:, None] * inv_freq, col[:, None] * inv_freq], axis=-1)
    cos = jnp.cos(ang).astype(x.dtype)[None, :, None, :]
    sin = jnp.sin(ang).astype(x.dtype)[None, :, None, :]

    lat, cx = x, ctx
    s1, s2 = GLA_COLS, GLA_COLS + SSD_COLS
    for l in range(DEPTH):
        last = l == DEPTH - 1
        mod_l = (jax.nn.silu(c) @ ada_w[l] + ada_b[l])[:, None, :]
        mod_c = (jax.nn.silu(c_ctx) @ ada_w[l] + ada_b[l])[None, None, :]
        sh1, sc1, gt1, sh2, sc2, gt2 = jnp.split(mod_l, 6, axis=-1)
        csh1, csc1, cgt1, csh2, csc2, cgt2 = jnp.split(mod_c, 6, axis=-1)

        p_l = modulate(rms_norm(lat, norm_mix_pre[l]), sh1, sc1) @ w_in[l]
        p_c = modulate(rms_norm(cx, norm_mix_pre[l]), csh1, csc1) @ w_in[l]
        gla_c, gla_l = gla_mixer(p_c[..., :s1], p_l[..., :s1],
                                 gla_gate_up[l], gla_gate_b[l], gla_norm[l])
        ssd_c, ssd_l = ssd_mixer(p_c[..., s1:s2], p_l[..., s1:s2], ssd_conv_w[l], ssd_conv_b[l],
                                 ssd_dt_bias[l], ssd_a_log[l], ssd_d[l], ssd_norm[l])
        ret_c, ret_l = retention_mixer(p_c[..., s2:], p_l[..., s2:], cos, sin, ret_norm[l])
        mixed_l = jnp.concatenate([gla_l, ssd_l, ret_l], axis=-1) @ w_out[l]
        lat = lat + gt1 * rms_norm(mixed_l, norm_mix_post[l])

        h_l = modulate(rms_norm(lat, norm_ffn_pre[l]), sh2, sc2)
        lat = lat + gt2 * rms_norm(swiglu(h_l, ffn_w13[l], ffn_w2[l]), norm_ffn_post[l])

        if not last:
            mixed_c = jnp.concatenate([gla_c, ssd_c, ret_c], axis=-1) @ w_out[l]
            cx = cx + cgt1 * rms_norm(mixed_c, norm_mix_post[l])
            h_c = modulate(rms_norm(cx, norm_ffn_pre[l]), csh2, csc2)
            cx = cx + cgt2 * rms_norm(swiglu(h_c, ffn_w13[l], ffn_w2[l]), norm_ffn_post[l])
    return lat
```

```python
import functools
import math

import jax
import jax.numpy as jnp
from jax import lax
from jax.experimental import pallas as pl
from jax.experimental.pallas import tpu as pltpu

F32 = jnp.float32
BF16 = jnp.bfloat16

D_MODEL = 1024
GRID_W = 64
RMS_EPS = 1e-6
GLA_W = 256
GLA_HEADS = 4
GLA_HEAD_V = 64
GLA_HEAD_K = 32
GLA_KDIM = 128
GLA_RANK = 16
GLA_GATE_TAU = 16.0
SSD_W = 512
SSD_HEADS = 8
SSD_HEAD_DIM = 64
SSD_GROUPS = 2
SSD_STATE = 128
SSD_CONV_W = 5
SSD_CONV_CH = 1024
RET_W = 256
RET_HEADS = 4
RET_HEAD_DIM = 64
ROPE_BASE = 10000.0
GLA_COLS = 800
SSD_COLS = 1552
FFN_HIDDEN = 2816

LANES = 128
SLAB = 256
TM_DENSE = 512
FFN_CHUNK = 256
GLA_CHUNK = 64
SSD_CHUNK = 128
RET_CHUNK = 128
SMALL_DT_LANE = 16
VMEM_LIMIT = 56 * 1024 * 1024


def _silu(x):
    return x * jax.nn.sigmoid(x)


def _dot(a, b):
    return jnp.dot(a, b, preferred_element_type=F32)


def _dot_nt(a, b):
    return lax.dot_general(a, b, (((1,), (1,)), ((), ())), preferred_element_type=F32)


def _dot_tn(a, b):
    return lax.dot_general(a, b, (((0,), (0,)), ((), ())), preferred_element_type=F32)


def _split_bf16(x, terms):
    parts = []
    r = x
    for _ in range(terms):
        hi = r.astype(BF16)
        parts.append(hi)
        r = r - hi.astype(F32)
    return parts


def _dot_sel_rhs(m01, x, terms=3):
    acc = None
    for p in _split_bf16(x, terms):
        t = _dot(m01, p)
        acc = t if acc is None else acc + t
    return acc


def _dot_sel_lhs(x, m01, terms=3):
    acc = None
    for p in _split_bf16(x, terms):
        t = _dot(p, m01)
        acc = t if acc is None else acc + t
    return acc


def _dot3(a, b):
    a_hi = a.astype(BF16)
    a_lo = (a - a_hi.astype(F32)).astype(BF16)
    b_hi = b.astype(BF16)
    b_lo = (b - b_hi.astype(F32)).astype(BF16)
    return _dot(a_hi, b_hi) + _dot(a_lo, b_hi) + _dot(a_hi, b_lo)


def _mod_row(slab, spb, ncs, nb):
    return jnp.where(slab % spb < ncs, nb, slab // spb)


def _rms(x, w):
    return x * lax.rsqrt(jnp.mean(x * x, axis=-1, keepdims=True) + RMS_EPS) * w


def _ada_kernel(cv_ref, w_ref, b_ref, o_ref):
    s = _silu(cv_ref[...]).astype(BF16)
    o_ref[...] = _dot(s, w_ref[...].astype(BF16)) + b_ref[...]


def _ada_call(cvec, ada_w, ada_b):
    depth, d, n6 = ada_w.shape
    tn = 2048
    return pl.pallas_call(
        _ada_kernel,
        grid=(depth, n6 // tn),
        in_specs=[
            pl.BlockSpec((8, d), lambda l, j: (0, 0)),
            pl.BlockSpec((None, d, tn), lambda l, j: (l, 0, j)),
            pl.BlockSpec((None, 1, tn), lambda l, j: (l, 0, j)),
        ],
        out_specs=pl.BlockSpec((None, 8, tn), lambda l, j: (l, 0, j)),
        out_shape=jax.ShapeDtypeStruct((depth, 8, n6), F32),
        compiler_params=pltpu.CompilerParams(
            dimension_semantics=("arbitrary", "arbitrary"), vmem_limit_bytes=VMEM_LIMIT),
        name="ada_mod",
    )(cvec, ada_w, ada_b.reshape(depth, 1, n6))


def _inproj_kernel(h_ref, mod_ref, nw_ref, wm_ref, ws_ref,
                   gla_ref, z_ref, xbc_ref, ret_ref, small_ref, yb_ref, *, tm, spb, ncs, nb):
    i = pl.program_id(0)
    d = D_MODEL
    for k in range(tm // SLAB):
        mrow = _mod_row(i * (tm // SLAB) + k, spb, ncs, nb)
        x = h_ref[k * SLAB:(k + 1) * SLAB, :]
        shift = mod_ref[mrow][:, 0:d]
        scale = mod_ref[mrow][:, d:2 * d]
        y = _rms(x, nw_ref[...]) * (1.0 + scale) + shift
        yb_ref[k * SLAB:(k + 1) * SLAB, :] = y.astype(BF16)
    yb = yb_ref[...]
    pm = _dot(yb, wm_ref[...])
    gla_ref[...] = pm[:, 0:768]
    z_ref[...] = pm[:, 768:1280]
    xbc_ref[...] = pm[:, 1280:2304]
    ret_ref[...] = pm[:, 2304:3328]
    ps = _dot(yb, ws_ref[...])
    small_ref[0] = ps[:, 0:LANES]
    small_ref[1] = ps[:, LANES:2 * LANES]


def _inproj_call(h, mod, nw, wm, ws, *, spb, ncs, nb):
    r, d = h.shape
    tm = TM_DENSE
    kern = functools.partial(_inproj_kernel, tm=tm, spb=spb, ncs=ncs, nb=nb)
    row = lambda i: (i, 0)
    const = lambda i: (0, 0)
    return pl.pallas_call(
        kern,
        grid=(r // tm,),
        in_specs=[
            pl.BlockSpec((tm, d), row),
            pl.BlockSpec(mod.shape, lambda i: (0, 0, 0)),
            pl.BlockSpec((1, d), const),
            pl.BlockSpec(wm.shape, const),
            pl.BlockSpec(ws.shape, const),
        ],
        out_specs=[
            pl.BlockSpec((tm, 768), row),
            pl.BlockSpec((tm, 512), row),
            pl.BlockSpec((tm, 1024), row),
            pl.BlockSpec((tm, 1024), row),
            pl.BlockSpec((2, tm, LANES), lambda i: (0, i, 0)),
        ],
        out_shape=[
            jax.ShapeDtypeStruct((r, 768), F32),
            jax.ShapeDtypeStruct((r, 512), F32),
            jax.ShapeDtypeStruct((r, 1024), F32),
            jax.ShapeDtypeStruct((r, 1024), F32),
            jax.ShapeDtypeStruct((2, r, LANES), F32),
        ],
        scratch_shapes=[pltpu.VMEM((tm, d), BF16)],
        compiler_params=pltpu.CompilerParams(
            dimension_semantics=("arbitrary",), vmem_limit_bytes=VMEM_LIMIT),
        name="in_proj",
    )(h, mod, nw, wm, ws)


def _conv_kernel(cur_ref, prev_ref, next_ref, w_ref, b_ref, o_ref, scr_ref, *, ncs, spb):
    t = pl.program_id(1)
    first = jnp.logical_or(t == 0, t == ncs)
    last = jnp.logical_or(t == ncs - 1, t == spb - 1)
    scr_ref[0:8, :] = jnp.where(first, 0.0, prev_ref[...])
    scr_ref[8:8 + SLAB, :] = cur_ref[...]
    scr_ref[8 + SLAB:16 + SLAB, :] = jnp.where(last, 0.0, next_ref[...])
    pad = (SSD_CONV_W - 1) // 2
    acc = b_ref[...] + w_ref[0:1, :] * scr_ref[pl.ds(8 - pad, SLAB), :]
    for k in range(1, SSD_CONV_W):
        acc = acc + w_ref[k:k + 1, :] * scr_ref[pl.ds(8 - pad + k, SLAB), :]
    o_ref[...] = _silu(acc)


def _conv_call(xbc, w, b, *, ncs):
    nb, rb, ch = xbc.shape
    spb = rb // SLAB
    nblk8 = rb // 8
    per = SLAB // 8
    kern = functools.partial(_conv_kernel, ncs=ncs, spb=spb)
    return pl.pallas_call(
        kern,
        grid=(nb, spb),
        in_specs=[
            pl.BlockSpec((None, SLAB, ch), lambda bb, t: (bb, t, 0)),
            pl.BlockSpec((None, 8, ch), lambda bb, t: (bb, jnp.maximum(t * per - 1, 0), 0)),
            pl.BlockSpec((None, 8, ch), lambda bb, t: (bb, jnp.minimum((t + 1) * per, nblk8 - 1), 0)),
            pl.BlockSpec((8, ch), lambda bb, t: (0, 0)),
            pl.BlockSpec((1, ch), lambda bb, t: (0, 0)),
        ],
        out_specs=pl.BlockSpec((None, SLAB, ch), lambda bb, t: (bb, t, 0)),
        out_shape=jax.ShapeDtypeStruct((nb, rb, ch), F32),
        scratch_shapes=[pltpu.VMEM((SLAB + 16, ch), F32)],
        compiler_params=pltpu.CompilerParams(
            dimension_semantics=("arbitrary", "arbitrary"), vmem_limit_bytes=VMEM_LIMIT),
        name="ssd_conv",
    )(xbc, xbc, xbc, w, b)


def _chunk_of(d, n, ncc, nch):
    back = jnp.where(n < ncc, ncc - 1 - n, nch + ncc - 1 - n)
    return jnp.where(d == 0, n, back)


def _out_chunk_of(d, n, ncc, nch):
    return jnp.where(d == 0, ncc - 1, _chunk_of(1, n, ncc, nch))


def _gla_kernel(p_ref, sm_ref, wg_ref, gb_ref, tri_ref, e2_ref, avg_ref, bdm_ref, nw_ref,
                o_ref, st_ref, oacc_ref, g_ref, *, c, ncc, nch):
    d = pl.program_id(1)
    n = pl.program_id(2)
    ck = _chunk_of(d, n, ncc, nch)
    sgn = 1 - 2 * d

    @pl.when(n == 0)
    def _():
        st_ref[...] = jnp.zeros_like(st_ref)

    z = _dot3(sm_ref[...], wg_ref[...]) + gb_ref[...]
    logg = (jnp.minimum(z, 0.0) - jnp.log1p(jnp.exp(-jnp.abs(z)))) * (1.0 / GLA_GATE_TAU)
    g = _dot_sel_rhs(tri_ref[...], logg)
    g_ref[...] = g
    q = p_ref[:, 0:GLA_KDIM] * (GLA_HEAD_K ** -0.5)
    k = p_ref[:, GLA_KDIM:2 * GLA_KDIM]
    v = p_ref[:, 2 * GLA_KDIM:2 * GLA_KDIM + GLA_W]
    g_last = jnp.where(d == 0, g[c - 1:c, :], g[0:1, :])

    st = st_ref[...]
    inter = _dot_nt((q * jnp.exp(g)).astype(BF16), st.astype(BF16))

    rows_s = lax.broadcasted_iota(jnp.int32, (c, GLA_KDIM), 0) * sgn

    def body(jb, acc):
        base = pl.multiple_of(jb * 8, 8)
        kblk = p_ref[pl.ds(base, 8), GLA_KDIM:2 * GLA_KDIM]
        vblk = p_ref[pl.ds(base, 8), 2 * GLA_KDIM:2 * GLA_KDIM + GLA_W]
        gblk = g_ref[pl.ds(base, 8), :]
        for r8 in range(8):
            j = base + r8
            dec = jnp.where(rows_s >= j * sgn,
                            jnp.exp(jnp.minimum(g - gblk[r8:r8 + 1, :], 0.0)), 0.0)
            pj = (dec * q * kblk[r8:r8 + 1, :]).astype(BF16)
            acc = acc + _dot(pj, e2_ref[...]) * vblk[r8:r8 + 1, :]
        return acc

    intra = lax.fori_loop(0, c // 8, body, jnp.zeros((c, GLA_W), F32))
    o = intra + inter

    kk = (k * jnp.exp(g_last - g)).astype(BF16)
    upd = _dot_tn(v.astype(BF16), kk)
    st_ref[...] = st * jnp.exp(g_last) + upd * bdm_ref[...]

    rows = pl.ds(pl.multiple_of(ck * c, c), c)

    @pl.when(d == 0)
    def _():
        oacc_ref[rows, :] = o

    @pl.when(d == 1)
    def _():
        tot = oacc_ref[rows, :] + o
        ms = _dot_sel_lhs(tot * tot, avg_ref[...], 2) * (1.0 / GLA_HEAD_V)
        y = tot * lax.rsqrt(ms + RMS_EPS) * nw_ref[...]
        r = p_ref[:, 2 * GLA_KDIM + GLA_W:2 * GLA_KDIM + 2 * GLA_W]
        o_ref[...] = (y * _silu(r)).astype(BF16)


def _gla_call(p_gla, small, wg, gb, nw, consts, *, nctx):
    nb, rb, _ = p_gla.shape
    c = GLA_CHUNK
    nch, ncc = rb // c, nctx // c
    kern = functools.partial(_gla_kernel, c=c, ncc=ncc, nch=nch)
    chunk = lambda b, d, n: (b, _chunk_of(d, n, ncc, nch), 0)
    const2 = lambda b, d, n: (0, 0)
    bydir = lambda b, d, n: (d, 0, 0)
    return pl.pallas_call(
        kern,
        grid=(nb, 2, nch),
        in_specs=[
            pl.BlockSpec((None, c, 768), chunk),
            pl.BlockSpec((None, None, c, LANES), lambda b, d, n: (d, b, _chunk_of(d, n, ncc, nch), 0)),
            pl.BlockSpec((None, LANES, GLA_KDIM), bydir),
            pl.BlockSpec((None, 1, GLA_KDIM), bydir),
            pl.BlockSpec((None, c, c), bydir),
            pl.BlockSpec((GLA_KDIM, GLA_W), const2),
            pl.BlockSpec((GLA_W, GLA_W), const2),
            pl.BlockSpec((GLA_W, GLA_KDIM), const2),
            pl.BlockSpec((1, GLA_W), const2),
        ],
        out_specs=pl.BlockSpec((None, c, GLA_W), lambda b, d, n: (b, _out_chunk_of(d, n, ncc, nch), 0)),
        out_shape=jax.ShapeDtypeStruct((nb, rb, GLA_W), BF16),
        scratch_shapes=[
            pltpu.VMEM((GLA_W, GLA_KDIM), F32),
            pltpu.VMEM((rb, GLA_W), F32),
            pltpu.VMEM((c, GLA_KDIM), F32),
        ],
        compiler_params=pltpu.CompilerParams(
            dimension_semantics=("arbitrary", "arbitrary", "arbitrary"), vmem_limit_bytes=VMEM_LIMIT),
        name="gla_scan",
    )(p_gla, small, wg, gb, consts["tri_gla"], consts["gla_e2"], consts["avg256"],
      consts["gla_bdm"], nw)


def _ssd_kernel(z_ref, xc_ref, sm_ref, dtb_ref, alog_ref, tri_ref, x_ref, hm_ref, dsk_ref, nw_ref,
                o_ref, st_ref, oacc_ref, *, c, ncc, nch):
    d = pl.program_id(1)
    n = pl.program_id(2)
    ck = _chunk_of(d, n, ncc, nch)
    sgn = 1 - 2 * d
    hpg = SSD_HEADS // SSD_GROUPS
    gw = hpg * SSD_HEAD_DIM

    @pl.when(n == 0)
    def _():
        st_ref[...] = jnp.zeros_like(st_ref)

    raw = sm_ref[...] + dtb_ref[...]
    dt = jnp.maximum(raw, 0.0) + jnp.log1p(jnp.exp(-jnp.abs(raw)))
    logg = dt * (-jnp.exp(alog_ref[...]))
    g = _dot_sel_rhs(tri_ref[...], logg)
    gt = g.T
    g_last = jnp.where(d == 0, g[c - 1:c, :], g[0:1, :])

    xs = xc_ref[:, 0:SSD_W]
    e_in = _dot_sel_lhs(jnp.exp(g), x_ref[...])
    dt_x = _dot_sel_lhs(dt, x_ref[...])
    w_st = _dot_sel_lhs(dt * jnp.exp(g_last - g), x_ref[...])
    gl_x = _dot_sel_lhs(jnp.broadcast_to(jnp.exp(g_last), (8, LANES)), x_ref[...])[0:1, :]

    ii = lax.broadcasted_iota(jnp.int32, (c, c), 0)
    jj = lax.broadcasted_iota(jnp.int32, (c, c), 1)
    tri_mask = (ii - jj) * sgn >= 0
    xw = xs * dt_x
    xst = (xs * w_st).astype(BF16)

    outs = []
    for grp in range(SSD_GROUPS):
        bm = xc_ref[:, SSD_W + grp * SSD_STATE:SSD_W + (grp + 1) * SSD_STATE].astype(BF16)
        cm = xc_ref[:, SSD_W + SSD_GROUPS * SSD_STATE + grp * SSD_STATE:
                    SSD_W + SSD_GROUPS * SSD_STATE + (grp + 1) * SSD_STATE].astype(BF16)
        cb = _dot_nt(cm, bm)
        xw_g = xw[:, grp * gw:(grp + 1) * gw]
        a_parts, x_parts = [], []
        for hh in range(hpg):
            lane = SMALL_DT_LANE + grp * hpg + hh
            gc = g[:, lane:lane + 1]
            gr = gt[lane:lane + 1, :]
            dec = jnp.where(tri_mask, jnp.exp(jnp.minimum(gc - gr, 0.0)), 0.0)
            a_parts.append((cb * dec).astype(BF16))
            x_parts.append((xw_g * hm_ref[hh:hh + 1, :]).astype(BF16))
        a_cat = jnp.concatenate(a_parts, axis=1)
        x_bd = jnp.concatenate(x_parts, axis=0)
        intra = _dot(a_cat, x_bd)
        st = st_ref[grp]
        inter = e_in[:, grp * gw:(grp + 1) * gw] * _dot(cm, st.astype(BF16))
        outs.append(intra + inter)
        st_ref[grp] = st * gl_x[:, grp * gw:(grp + 1) * gw] + _dot_tn(bm, xst[:, grp * gw:(grp + 1) * gw])
    o = jnp.concatenate(outs, axis=1)

    rows = pl.ds(pl.multiple_of(ck * c, c), c)

    @pl.when(d == 0)
    def _():
        oacc_ref[rows, :] = o

    @pl.when(d == 1)
    def _():
        y = oacc_ref[rows, :] + o + dsk_ref[...] * xs
        y = y * _silu(z_ref[...])
        o_ref[...] = _rms(y, nw_ref[...]).astype(BF16)


def _ssd_call(p_z, xconv, small, dtb, alog, dsk, nw, consts, *, nctx):
    nb, rb, _ = p_z.shape
    c = SSD_CHUNK
    nch, ncc = rb // c, nctx // c
    kern = functools.partial(_ssd_kernel, c=c, ncc=ncc, nch=nch)
    chunk = lambda b, d, n: (b, _chunk_of(d, n, ncc, nch), 0)
    const2 = lambda b, d, n: (0, 0)
    bydir = lambda b, d, n: (d, 0, 0)
    return pl.pallas_call(
        kern,
        grid=(nb, 2, nch),
        in_specs=[
            pl.BlockSpec((None, c, SSD_W), chunk),
            pl.BlockSpec((None, c, SSD_CONV_CH), chunk),
            pl.BlockSpec((None, None, c, LANES), lambda b, d, n: (d, b, _chunk_of(d, n, ncc, nch), 0)),
            pl.BlockSpec((None, 1, LANES), bydir),
            pl.BlockSpec((None, 1, LANES), bydir),
            pl.BlockSpec((None, c, c), bydir),
            pl.BlockSpec((LANES, SSD_W), const2),
            pl.BlockSpec((8, 256), const2),
            pl.BlockSpec((1, SSD_W), const2),
            pl.BlockSpec((1, SSD_W), const2),
        ],
        out_specs=pl.BlockSpec((None, c, SSD_W), lambda b, d, n: (b, _out_chunk_of(d, n, ncc, nch), 0)),
        out_shape=jax.ShapeDtypeStruct((nb, rb, SSD_W), BF16),
        scratch_shapes=[
            pltpu.VMEM((SSD_GROUPS, SSD_STATE, 256), F32),
            pltpu.VMEM((rb, SSD_W), F32),
        ],
        compiler_params=pltpu.CompilerParams(
            dimension_semantics=("arbitrary", "arbitrary", "arbitrary"), vmem_limit_bytes=VMEM_LIMIT),
        name="ssd_scan",
    )(p_z, xconv, small, dtb, alog, consts["tri_ssd"], consts["ssd_x"], consts["hmask"], dsk, nw)


def _ret_log_gamma_row(width):
    lane = lax.broadcasted_iota(jnp.int32, (1, width), 1)
    row = jnp.zeros((1, width), F32)
    for h in range(RET_HEADS):
        lg = math.log1p(-(2.0 ** (-5.0 - h)))
        row = jnp.where(lane // RET_HEAD_DIM == h, lg, row)
    return row


def _rope(t, cos, sin_signed):
    lane = lax.broadcasted_iota(jnp.int32, t.shape, 1)
    half = RET_HEAD_DIM // 2
    swapped = jnp.where(lane % RET_HEAD_DIM < half,
                        pltpu.roll(t, LANES - half, 1), pltpu.roll(t, half, 1))
    return t * cos + swapped * sin_signed


def _ret_kernel(p_ref, cos_ref, sin_ref, avg_ref, hm_ref, nw_ref,
                o_ref, st_ref, oacc_ref, *, c, ncc, nch):
    d = pl.program_id(1)
    n = pl.program_id(2)
    ck = _chunk_of(d, n, ncc, nch)
    sgn = 1 - 2 * d
    w = RET_W

    @pl.when(n == 0)
    def _():
        st_ref[...] = jnp.zeros_like(st_ref)

    cos = cos_ref[...]
    sin = sin_ref[...]
    q_halves, k_halves = [], []
    for hf in range(w // LANES):
        qh = p_ref[:, hf * LANES:(hf + 1) * LANES] * (RET_HEAD_DIM ** -0.5)
        kh = p_ref[:, w + hf * LANES:w + (hf + 1) * LANES]
        q_halves.append(_rope(qh, cos, sin))
        k_halves.append(_rope(kh, cos, sin))
    q = jnp.concatenate(q_halves, axis=1)
    k = jnp.concatenate(k_halves, axis=1)
    v = p_ref[:, 2 * w:3 * w]

    lg_row = _ret_log_gamma_row(w)
    pos = lax.broadcasted_iota(jnp.int32, (c, 1), 0)
    cnt_in = jnp.where(d == 0, pos + 1, c - pos).astype(F32)
    cnt_st = jnp.where(d == 0, c - 1 - pos, pos).astype(F32)
    e_in = jnp.exp(cnt_in * lg_row)
    e_st = jnp.exp(cnt_st * lg_row)
    e_all = jnp.exp(float(c) * lg_row)

    ii = lax.broadcasted_iota(jnp.int32, (c, c), 0)
    jj = lax.broadcasted_iota(jnp.int32, (c, c), 1)
    dist = (ii - jj) * sgn
    qb = q.astype(BF16)
    k_parts, v_parts, dec_parts = [], [], []
    for h in range(RET_HEADS):
        lg = math.log1p(-(2.0 ** (-5.0 - h)))
        hm = hm_ref[h:h + 1, :]
        k_parts.append((k * hm).astype(BF16))
        v_parts.append((v * hm).astype(BF16))
        dec_parts.append(jnp.where(dist >= 0, jnp.exp(dist.astype(F32) * lg), 0.0))
    k_bd = jnp.concatenate(k_parts, axis=0)
    v_bd = jnp.concatenate(v_parts, axis=0)
    dec = jnp.concatenate(dec_parts, axis=1)
    scores = _dot_nt(qb, k_bd) * dec
    intra = _dot(scores.astype(BF16), v_bd)
    st = st_ref[...]
    inter = e_in * _dot(qb, st.astype(BF16))
    o = intra + inter
    upd = _dot_tn((k * e_st).astype(BF16), v.astype(BF16))
    st_ref[...] = st * e_all + upd * avg_ref[...].astype(F32)

    rows = pl.ds(pl.multiple_of(ck * c, c), c)

    @pl.when(d == 0)
    def _():
        oacc_ref[rows, :] = o

    @pl.when(d == 1)
    def _():
        tot = oacc_ref[rows, :] + o
        mu = _dot_sel_lhs(tot, avg_ref[...], 2) * (1.0 / RET_HEAD_DIM)
        xc = tot - mu
        var = _dot_sel_lhs(xc * xc, avg_ref[...], 2) * (1.0 / RET_HEAD_DIM)
        y = xc * lax.rsqrt(var + RMS_EPS) * nw_ref[...]
        o_ref[...] = (y * _silu(p_ref[:, 3 * w:4 * w])).astype(BF16)


def _ret_call(p_ret, cos_t, sin_t, nw, consts, *, nctx):
    nb, rb, _ = p_ret.shape
    c = RET_CHUNK
    nch, ncc = rb // c, nctx // c
    kern = functools.partial(_ret_kernel, c=c, ncc=ncc, nch=nch)
    chunk = lambda b, d, n: (b, _chunk_of(d, n, ncc, nch), 0)
    tchunk = lambda b, d, n: (_chunk_of(d, n, ncc, nch), 0)
    const2 = lambda b, d, n: (0, 0)
    return pl.pallas_call(
        kern,
        grid=(nb, 2, nch),
        in_specs=[
            pl.BlockSpec((None, c, 4 * RET_W), chunk),
            pl.BlockSpec((c, LANES), tchunk),
            pl.BlockSpec((c, LANES), tchunk),
            pl.BlockSpec((RET_W, RET_W), const2),
            pl.BlockSpec((8, 256), const2),
            pl.BlockSpec((1, RET_W), const2),
        ],
        out_specs=pl.BlockSpec((None, c, RET_W), lambda b, d, n: (b, _out_chunk_of(d, n, ncc, nch), 0)),
        out_shape=jax.ShapeDtypeStruct((nb, rb, RET_W), BF16),
        scratch_shapes=[
            pltpu.VMEM((RET_W, RET_W), F32),
            pltpu.VMEM((rb, RET_W), F32),
        ],
        compiler_params=pltpu.CompilerParams(
            dimension_semantics=("arbitrary", "arbitrary", "arbitrary"), vmem_limit_bytes=VMEM_LIMIT),
        name="ret_scan",
    )(p_ret, cos_t, sin_t, consts["avg256"], consts["hmask"], nw)


def _outproj_kernel(h_ref, mod_ref, gla_ref, ssd_ref, ret_ref, wo_ref, nw_ref, o_ref, *, tm, spb, ncs, nb):
    i = pl.program_id(0)
    d = D_MODEL
    mixed = jnp.concatenate([gla_ref[...], ssd_ref[...], ret_ref[...]], axis=1)
    m = _dot(mixed, wo_ref[...])
    for k in range(tm // SLAB):
        mrow = _mod_row(i * (tm // SLAB) + k, spb, ncs, nb)
        gate = mod_ref[mrow][:, 2 * d:3 * d]
        sl = slice(k * SLAB, (k + 1) * SLAB)
        o_ref[sl, :] = h_ref[sl, :] + gate * _rms(m[sl, :], nw_ref[...])


def _outproj_call(h, mod, gla, ssd, ret, wo, nw, *, spb, ncs, nb):
    r, d = h.shape
    tm = TM_DENSE
    kern = functools.partial(_outproj_kernel, tm=tm, spb=spb, ncs=ncs, nb=nb)
    row = lambda i: (i, 0)
    const = lambda i: (0, 0)
    return pl.pallas_call(
        kern,
        grid=(r // tm,),
        in_specs=[
            pl.BlockSpec((tm, d), row),
            pl.BlockSpec(mod.shape, lambda i: (0, 0, 0)),
            pl.BlockSpec((tm, GLA_W), row),
            pl.BlockSpec((tm, SSD_W), row),
            pl.BlockSpec((tm, RET_W), row),
            pl.BlockSpec(wo.shape, const),
            pl.BlockSpec((1, d), const),
        ],
        out_specs=pl.BlockSpec((tm, d), row),
        out_shape=jax.ShapeDtypeStruct((r, d), F32),
        compiler_params=pltpu.CompilerParams(
            dimension_semantics=("arbitrary",), vmem_limit_bytes=VMEM_LIMIT),
        name="out_proj",
    )(h, mod, gla, ssd, ret, wo, nw)


def _ffn_kernel(h_ref, mod_ref, npre_ref, w13_ref, w2_ref, npost_ref, o_ref, hb_ref, acc_ref,
                *, tm, spb, ncs, nb):
    i = pl.program_id(0)
    d = D_MODEL
    for k in range(tm // SLAB):
        mrow = _mod_row(i * (tm // SLAB) + k, spb, ncs, nb)
        shift = mod_ref[mrow][:, 3 * d:4 * d]
        scale = mod_ref[mrow][:, 4 * d:5 * d]
        sl = slice(k * SLAB, (k + 1) * SLAB)
        y = _rms(h_ref[sl, :], npre_ref[...]) * (1.0 + scale) + shift
        hb_ref[sl, :] = y.astype(BF16)
    hb = hb_ref[...]
    for cidx in range(FFN_HIDDEN // FFN_CHUNK):
        lo = cidx * FFN_CHUNK
        gate = _dot(hb, w13_ref[:, lo:lo + FFN_CHUNK])
        up = _dot(hb, w13_ref[:, FFN_HIDDEN + lo:FFN_HIDDEN + lo + FFN_CHUNK])
        act = (_silu(gate) * up).astype(BF16)
        part = _dot(act, w2_ref[lo:lo + FFN_CHUNK, :])
        if cidx == 0:
            acc_ref[...] = part
        else:
            acc_ref[...] += part
    for k in range(tm // SLAB):
        mrow = _mod_row(i * (tm // SLAB) + k, spb, ncs, nb)
        gate = mod_ref[mrow][:, 5 * d:6 * d]
        sl = slice(k * SLAB, (k + 1) * SLAB)
        o_ref[sl, :] = h_ref[sl, :] + gate * _rms(acc_ref[sl, :], npost_ref[...])


def _ffn_call(h, mod, npre, w13, w2, npost, *, spb, ncs, nb):
    r, d = h.shape
    tm = TM_DENSE
    kern = functools.partial(_ffn_kernel, tm=tm, spb=spb, ncs=ncs, nb=nb)
    row = lambda i: (i, 0)
    const = lambda i: (0, 0)
    return pl.pallas_call(
        kern,
        grid=(r // tm,),
        in_specs=[
            pl.BlockSpec((tm, d), row),
            pl.BlockSpec(mod.shape, lambda i: (0, 0, 0)),
            pl.BlockSpec((1, d), const),
            pl.BlockSpec(w13.shape, const),
            pl.BlockSpec(w2.shape, const),
            pl.BlockSpec((1, d), const),
        ],
        out_specs=pl.BlockSpec((tm, d), row),
        out_shape=jax.ShapeDtypeStruct((r, d), F32),
        scratch_shapes=[pltpu.VMEM((tm, d), BF16), pltpu.VMEM((tm, d), F32)],
        compiler_params=pltpu.CompilerParams(
            dimension_semantics=("arbitrary",), vmem_limit_bytes=VMEM_LIMIT),
        name="ffn",
    )(h, mod, npre, w13, w2, npost)


def _tri_pair(c):
    i = jnp.arange(c)[:, None]
    j = jnp.arange(c)[None, :]
    return jnp.stack([(j <= i), (j >= i)]).astype(BF16)


def _constants():
    lane256 = jnp.arange(256)
    consts = {
        "tri_gla": _tri_pair(GLA_CHUNK),
        "tri_ssd": _tri_pair(SSD_CHUNK),
        "avg256": (lane256[:, None] // 64 == lane256[None, :] // 64).astype(BF16),
        "gla_e2": (jnp.arange(GLA_KDIM)[:, None] // GLA_HEAD_K == lane256[None, :] // GLA_HEAD_V).astype(BF16),
        "gla_bdm": (lane256[:, None] // GLA_HEAD_V == jnp.arange(GLA_KDIM)[None, :] // GLA_HEAD_K).astype(F32),
        "ssd_x": (jnp.arange(LANES)[:, None] - SMALL_DT_LANE
                  == jnp.arange(SSD_W)[None, :] // SSD_HEAD_DIM).astype(BF16),
        "hmask": (jnp.arange(8)[:, None] == lane256[None, :] // 64).astype(F32),
    }
    return consts


def _rope_tables(t_lat, nctx):
    rows = t_lat // GRID_W
    row = jnp.repeat(jnp.arange(rows), GRID_W).astype(F32)
    col = jnp.tile(jnp.arange(GRID_W), rows).astype(F32)
    n_freq = RET_HEAD_DIM // 4
    inv_freq = ROPE_BASE ** (-jnp.arange(n_freq, dtype=F32) / n_freq)
    ang = jnp.concatenate([row[:, None] * inv_freq, col[:, None] * inv_freq], axis=-1)
    cos = jnp.cos(ang)
    sin = jnp.sin(ang)
    cos_t = jnp.concatenate([cos, cos, cos, cos], axis=-1)
    sin_t = jnp.concatenate([-sin, sin, -sin, sin], axis=-1)
    cos_t = jnp.concatenate([jnp.ones((nctx, LANES), F32), cos_t], axis=0)
    sin_t = jnp.concatenate([jnp.zeros((nctx, LANES), F32), sin_t], axis=0)
    return cos_t, sin_t


def kernel(x, c, ctx, c_ctx, ada_w, ada_b, norm_mix_pre, norm_mix_post, norm_ffn_pre, norm_ffn_post,
           w_in, w_out, gla_gate_up, gla_gate_b, gla_norm, ssd_conv_w, ssd_conv_b, ssd_dt_bias,
           ssd_a_log, ssd_d, ssd_norm, ret_norm, ffn_w13, ffn_w2):
    nb, t_lat, d = x.shape
    nctx = ctx.shape[1]
    depth = ada_w.shape[0]
    rb = nctx + t_lat
    r = nb * rb
    spb = rb // SLAB
    ncs = nctx // SLAB
    assert d == D_MODEL and nb + 1 <= 8
    assert nctx % SLAB == 0 and t_lat % SLAB == 0 and r % TM_DENSE == 0

    consts = _constants()
    cos_t, sin_t = _rope_tables(t_lat, nctx)

    cvec = jnp.zeros((8, d), F32).at[:nb].set(c).at[nb].set(c_ctx)
    mods = _ada_call(cvec, ada_w, ada_b)

    s1, s2 = GLA_COLS, GLA_COLS + SSD_COLS
    lr0 = 2 * GLA_KDIM + 2 * GLA_W
    dt0 = s1 + SSD_W + SSD_CONV_CH
    wm = jnp.concatenate([w_in[:, :, 0:lr0], w_in[:, :, s1:dt0], w_in[:, :, s2:]], axis=-1).astype(BF16)
    zpad = jnp.zeros((depth, d, LANES - GLA_RANK - SSD_HEADS), F32)
    ws = jnp.concatenate(
        [w_in[:, :, lr0:lr0 + GLA_RANK], w_in[:, :, dt0:dt0 + SSD_HEADS], zpad,
         w_in[:, :, lr0 + GLA_RANK:lr0 + 2 * GLA_RANK], w_in[:, :, dt0 + SSD_HEADS:dt0 + 2 * SSD_HEADS], zpad],
        axis=-1).astype(BF16)
    wo = w_out.astype(BF16)
    w13 = ffn_w13.astype(BF16)
    w2 = ffn_w2.astype(BF16)

    wg = jnp.zeros((depth, 2, LANES, GLA_KDIM), F32).at[:, :, :GLA_RANK, :].set(gla_gate_up)
    gb = gla_gate_b.reshape(depth, 2, 1, GLA_KDIM)
    lane_dt = slice(SMALL_DT_LANE, SMALL_DT_LANE + SSD_HEADS)
    dtb = jnp.zeros((depth, 2, 1, LANES), F32).at[:, :, 0, lane_dt].set(ssd_dt_bias)
    alog = jnp.zeros((depth, 2, 1, LANES), F32).at[:, :, 0, lane_dt].set(ssd_a_log)
    dsk = jnp.repeat(ssd_d, SSD_HEAD_DIM, axis=-1).reshape(depth, 1, SSD_W)
    conv_w = jnp.zeros((depth, 8, SSD_CONV_CH), F32).at[:, :SSD_CONV_W].set(ssd_conv_w)

    h = jnp.concatenate([ctx, x], axis=1).reshape(r, d)
    meta = dict(spb=spb, ncs=ncs, nb=nb)
    row1 = lambda a: a.reshape(1, -1)
    for l in range(depth):
        mod = mods[l].reshape(8, 1, 6 * d)
        p_gla, p_z, p_xbc, p_ret, small = _inproj_call(h, mod, row1(norm_mix_pre[l]), wm[l], ws[l], **meta)
        small = small.reshape(2, nb, rb, LANES)
        xconv = _conv_call(p_xbc.reshape(nb, rb, SSD_CONV_CH), conv_w[l], row1(ssd_conv_b[l]), ncs=ncs)
        gla = _gla_call(p_gla.reshape(nb, rb, 768), small, wg[l], gb[l], row1(gla_norm[l]), consts, nctx=nctx)
        ssd = _ssd_call(p_z.reshape(nb, rb, SSD_W), xconv, small, dtb[l], alog[l], dsk[l],
                        row1(ssd_norm[l]), consts, nctx=nctx)
        ret = _ret_call(p_ret.reshape(nb, rb, 4 * RET_W), cos_t, sin_t, row1(ret_norm[l]), consts, nctx=nctx)
        h = _outproj_call(h, mod, gla.reshape(r, GLA_W), ssd.reshape(r, SSD_W), ret.reshape(r, RET_W),
                          wo[l], row1(norm_mix_post[l]), **meta)
        h = _ffn_call(h, mod, row1(norm_ffn_pre[l]), w13[l], w2[l], row1(norm_ffn_post[l]), **meta)
    return h.reshape(nb, rb, d)[:, nctx:, :]
```

```python
import functools
import math

import jax
import jax.numpy as jnp
from jax import lax
from jax.experimental import pallas as pl
from jax.experimental.pallas import tpu as pltpu

F32 = jnp.float32
BF16 = jnp.bfloat16

D_MODEL = 1024
GRID_W = 64
RMS_EPS = 1e-6
GLA_W = 256
GLA_HEADS = 4
GLA_HEAD_V = 64
GLA_HEAD_K = 32
GLA_KDIM = 128
GLA_RANK = 16
GLA_GATE_TAU = 16.0
SSD_W = 512
SSD_HEADS = 8
SSD_HEAD_DIM = 64
SSD_GROUPS = 2
SSD_STATE = 128
SSD_CONV_W = 5
SSD_CONV_CH = 1024
RET_W = 256
RET_HEADS = 4
RET_HEAD_DIM = 64
ROPE_BASE = 10000.0
GLA_COLS = 800
SSD_COLS = 1552
FFN_HIDDEN = 2816

LANES = 128
SLAB = 256
TM_DENSE = 512
FFN_CHUNK = 256
GLA_SUB = 16
GLA_BLOCK = 128
SSD_CHUNK = 128
RET_CHUNK = 128
SMALL_DT_LANE = 16
VMEM_LIMIT = 56 * 1024 * 1024


def _silu(x):
    return x * jax.nn.sigmoid(x)


def _dot(a, b):
    return jnp.dot(a, b, preferred_element_type=F32)


def _dot_nt(a, b):
    return lax.dot_general(a, b, (((1,), (1,)), ((), ())), preferred_element_type=F32)


def _dot_tn(a, b):
    return lax.dot_general(a, b, (((0,), (0,)), ((), ())), preferred_element_type=F32)


def _split_bf16(x, terms):
    parts = []
    r = x
    for _ in range(terms):
        hi = r.astype(BF16)
        parts.append(hi)
        r = r - hi.astype(F32)
    return parts


def _dot_sel_rhs(m01, x, terms=3):
    acc = None
    for p in _split_bf16(x, terms):
        t = _dot(m01, p)
        acc = t if acc is None else acc + t
    return acc


def _dot_sel_lhs(x, m01, terms=3):
    acc = None
    for p in _split_bf16(x, terms):
        t = _dot(p, m01)
        acc = t if acc is None else acc + t
    return acc


def _dot3(a, b):
    a_hi = a.astype(BF16)
    a_lo = (a - a_hi.astype(F32)).astype(BF16)
    b_hi = b.astype(BF16)
    b_lo = (b - b_hi.astype(F32)).astype(BF16)
    return _dot(a_hi, b_hi) + _dot(a_lo, b_hi) + _dot(a_hi, b_lo)


def _mod_row(slab, spb, ncs, nb):
    return jnp.where(slab % spb < ncs, nb, slab // spb)


def _rms(x, w):
    return x * lax.rsqrt(jnp.mean(x * x, axis=-1, keepdims=True) + RMS_EPS) * w


def _ada_kernel(cv_ref, w_ref, b_ref, o_ref):
    s = _silu(cv_ref[...]).astype(BF16)
    o_ref[...] = _dot(s, w_ref[...].astype(BF16)) + b_ref[...]


def _ada_call(cvec, ada_w, ada_b):
    depth, d, n6 = ada_w.shape
    tn = 2048
    return pl.pallas_call(
        _ada_kernel,
        grid=(depth, n6 // tn),
        in_specs=[
            pl.BlockSpec((8, d), lambda l, j: (0, 0)),
            pl.BlockSpec((None, d, tn), lambda l, j: (l, 0, j)),
            pl.BlockSpec((None, 1, tn), lambda l, j: (l, 0, j)),
        ],
        out_specs=pl.BlockSpec((None, 8, tn), lambda l, j: (l, 0, j)),
        out_shape=jax.ShapeDtypeStruct((depth, 8, n6), F32),
        compiler_params=pltpu.CompilerParams(
            dimension_semantics=("arbitrary", "arbitrary"), vmem_limit_bytes=VMEM_LIMIT),
        name="ada_mod",
    )(cvec, ada_w, ada_b.reshape(depth, 1, n6))


def _inproj_kernel(h_ref, mod_ref, nw_ref, wm_ref, ws_ref,
                   gla_ref, z_ref, xbc_ref, ret_ref, small_ref, yb_ref, *, tm, spb, ncs, nb):
    i = pl.program_id(0)
    d = D_MODEL
    for k in range(tm // SLAB):
        mrow = _mod_row(i * (tm // SLAB) + k, spb, ncs, nb)
        x = h_ref[k * SLAB:(k + 1) * SLAB, :]
        shift = mod_ref[mrow][:, 0:d]
        scale = mod_ref[mrow][:, d:2 * d]
        y = _rms(x, nw_ref[...]) * (1.0 + scale) + shift
        yb_ref[k * SLAB:(k + 1) * SLAB, :] = y.astype(BF16)
    yb = yb_ref[...]
    pm = _dot(yb, wm_ref[...])
    gla_ref[...] = pm[:, 0:768]
    z_ref[...] = pm[:, 768:1280]
    xbc_ref[...] = pm[:, 1280:2304]
    ret_ref[...] = pm[:, 2304:3328]
    ps = _dot(yb, ws_ref[...])
    small_ref[0] = ps[:, 0:LANES]
    small_ref[1] = ps[:, LANES:2 * LANES]


def _inproj_call(h, mod, nw, wm, ws, *, spb, ncs, nb):
    r, d = h.shape
    tm = TM_DENSE
    kern = functools.partial(_inproj_kernel, tm=tm, spb=spb, ncs=ncs, nb=nb)
    row = lambda i: (i, 0)
    const = lambda i: (0, 0)
    return pl.pallas_call(
        kern,
        grid=(r // tm,),
        in_specs=[
            pl.BlockSpec((tm, d), row),
            pl.BlockSpec(mod.shape, lambda i: (0, 0, 0)),
            pl.BlockSpec((1, d), const),
            pl.BlockSpec(wm.shape, const),
            pl.BlockSpec(ws.shape, const),
        ],
        out_specs=[
            pl.BlockSpec((tm, 768), row),
            pl.BlockSpec((tm, 512), row),
            pl.BlockSpec((tm, 1024), row),
            pl.BlockSpec((tm, 1024), row),
            pl.BlockSpec((2, tm, LANES), lambda i: (0, i, 0)),
        ],
        out_shape=[
            jax.ShapeDtypeStruct((r, 768), F32),
            jax.ShapeDtypeStruct((r, 512), F32),
            jax.ShapeDtypeStruct((r, 1024), F32),
            jax.ShapeDtypeStruct((r, 1024), F32),
            jax.ShapeDtypeStruct((2, r, LANES), F32),
        ],
        scratch_shapes=[pltpu.VMEM((tm, d), BF16)],
        compiler_params=pltpu.CompilerParams(
            dimension_semantics=("arbitrary",), vmem_limit_bytes=VMEM_LIMIT),
        name="in_proj",
    )(h, mod, nw, wm, ws)


def _conv_kernel(cur_ref, prev_ref, next_ref, w_ref, b_ref, o_ref, scr_ref, *, ncs, spb):
    t = pl.program_id(1)
    first = jnp.logical_or(t == 0, t == ncs)
    last = jnp.logical_or(t == ncs - 1, t == spb - 1)
    scr_ref[0:8, :] = jnp.where(first, 0.0, prev_ref[...])
    scr_ref[8:8 + SLAB, :] = cur_ref[...]
    scr_ref[8 + SLAB:16 + SLAB, :] = jnp.where(last, 0.0, next_ref[...])
    pad = (SSD_CONV_W - 1) // 2
    acc = b_ref[...] + w_ref[0:1, :] * scr_ref[pl.ds(8 - pad, SLAB), :]
    for k in range(1, SSD_CONV_W):
        acc = acc + w_ref[k:k + 1, :] * scr_ref[pl.ds(8 - pad + k, SLAB), :]
    o_ref[...] = _silu(acc)


def _conv_call(xbc, w, b, *, ncs):
    nb, rb, ch = xbc.shape
    spb = rb // SLAB
    nblk8 = rb // 8
    per = SLAB // 8
    kern = functools.partial(_conv_kernel, ncs=ncs, spb=spb)
    return pl.pallas_call(
        kern,
        grid=(nb, spb),
        in_specs=[
            pl.BlockSpec((None, SLAB, ch), lambda bb, t: (bb, t, 0)),
            pl.BlockSpec((None, 8, ch), lambda bb, t: (bb, jnp.maximum(t * per - 1, 0), 0)),
            pl.BlockSpec((None, 8, ch), lambda bb, t: (bb, jnp.minimum((t + 1) * per, nblk8 - 1), 0)),
            pl.BlockSpec((8, ch), lambda bb, t: (0, 0)),
            pl.BlockSpec((1, ch), lambda bb, t: (0, 0)),
        ],
        out_specs=pl.BlockSpec((None, SLAB, ch), lambda bb, t: (bb, t, 0)),
        out_shape=jax.ShapeDtypeStruct((nb, rb, ch), F32),
        scratch_shapes=[pltpu.VMEM((SLAB + 16, ch), F32)],
        compiler_params=pltpu.CompilerParams(
            dimension_semantics=("arbitrary", "arbitrary"), vmem_limit_bytes=VMEM_LIMIT),
        name="ssd_conv",
    )(xbc, xbc, xbc, w, b)


def _chunk_of(direction, n, ncc, nch):
    if direction == 0:
        return n
    return jnp.where(n < ncc, ncc - 1 - n, nch + ncc - 1 - n)


def _scan_params():
    return pltpu.CompilerParams(dimension_semantics=("arbitrary",), vmem_limit_bytes=VMEM_LIMIT)


def _gla_kernel(*refs, direction, nb, cg):
    if direction == 0:
        (p_ref, sm_ref, wg_ref, gb_ref, cum_ref, e2_ref, bdm_ref,
         o_ref, st_ref, qs_ref, g_ref, kk_ref, egl_ref) = refs
        dst_ref = o_ref
    else:
        (p_ref, sm_ref, of_ref, wg_ref, gb_ref, cum_ref, e2_ref, bdm_ref, avg_ref, nw_ref,
         o_ref, st_ref, qs_ref, g_ref, kk_ref, egl_ref, oc_ref) = refs
        dst_ref = oc_ref
    sb = GLA_SUB
    nsb = cg // sb

    @pl.when(pl.program_id(0) == 0)
    def _():
        st_ref[...] = jnp.zeros_like(st_ref)

    for b in range(nb):
        z = _dot3(sm_ref[b], wg_ref[...]) + gb_ref[...]
        logg = (jnp.minimum(z, 0.0) - jnp.log1p(jnp.exp(-jnp.abs(z)))) * (1.0 / GLA_GATE_TAU)
        cums = _dot_sel_rhs(cum_ref[...], logg)
        gs = cums[0:cg]
        gl = cums[cg:2 * cg]
        g_ref[b] = gs
        egl_ref[b] = jnp.exp(gl)
        qs_ref[b] = p_ref[b, :, 0:GLA_KDIM] * (GLA_HEAD_K ** -0.5)
        kk_ref[b] = p_ref[b, :, GLA_KDIM:2 * GLA_KDIM] * jnp.exp(gl - gs)

    rows16 = lax.broadcasted_iota(jnp.int32, (sb, GLA_KDIM), 0)
    masks = [(rows16 >= j) if direction == 0 else (rows16 <= j) for j in range(sb)]

    def body(it, carry):
        a = it if direction == 0 else nsb - 1 - it
        rs = pl.ds(pl.multiple_of(a * sb, sb), sb)
        for b in range(nb):
            gs = g_ref[b, rs, :]
            qs = qs_ref[b, rs, :]
            kb = p_ref[b, rs, GLA_KDIM:2 * GLA_KDIM]
            vb = p_ref[b, rs, 2 * GLA_KDIM:2 * GLA_KDIM + GLA_W]
            st = st_ref[b]
            inter = _dot_nt((qs * jnp.exp(gs)).astype(BF16), st.astype(BF16))
            parts = []
            for j in range(sb):
                dec = jnp.where(masks[j], jnp.exp(jnp.minimum(gs - gs[j:j + 1, :], 0.0)), 0.0)
                parts.append((dec * qs * kb[j:j + 1, :]).astype(BF16))
            sx = _dot(jnp.concatenate(parts, axis=0), e2_ref[...])
            intra = sx[0:sb] * vb[0:1, :]
            for j in range(1, sb):
                intra = intra + sx[j * sb:(j + 1) * sb] * vb[j:j + 1, :]
            dst_ref[b, rs, :] = intra + inter
            upd = _dot_tn(vb.astype(BF16), kk_ref[b, rs, :].astype(BF16))
            st_ref[b] = st * egl_ref[b, rs, :][0:1, :] + upd * bdm_ref[...]
        return carry

    lax.fori_loop(0, nsb, body, 0)

    if direction == 1:
        for b in range(nb):
            tot = of_ref[b] + oc_ref[b]
            ms = _dot_sel_lhs(tot * tot, avg_ref[...], 2) * (1.0 / GLA_HEAD_V)
            y = tot * lax.rsqrt(ms + RMS_EPS) * nw_ref[...]
            r = p_ref[b, :, 2 * GLA_KDIM + GLA_W:2 * GLA_KDIM + 2 * GLA_W]
            o_ref[b] = (y * _silu(r)).astype(BF16)


def _gla_call(direction, p_gla, small_d, o_f, wg_d, gb_d, nw, consts, *, nctx):
    nb, rb, _ = p_gla.shape
    cg = GLA_BLOCK
    nch, ncc = rb // cg, nctx // cg
    kern = functools.partial(_gla_kernel, direction=direction, nb=nb, cg=cg)
    chunk = lambda n: (0, _chunk_of(direction, n, ncc, nch), 0)
    const2 = lambda n: (0, 0)
    blk = lambda w: pl.BlockSpec((nb, cg, w), chunk)
    full = lambda a: pl.BlockSpec(a.shape, const2)
    cum = consts["gla_cum"][direction]
    ins = [p_gla, small_d]
    specs = [blk(768), blk(LANES)]
    if direction == 1:
        ins.append(o_f)
        specs.append(blk(GLA_W))
    tail = [wg_d, gb_d, cum, consts["gla_e2"], consts["gla_bdm"]]
    if direction == 1:
        tail += [consts["avg256"], nw]
    ins += tail
    specs += [full(a) for a in tail]
    scratch = [
        pltpu.VMEM((nb, GLA_W, GLA_KDIM), F32),
        pltpu.VMEM((nb, cg, GLA_KDIM), F32),
        pltpu.VMEM((nb, cg, GLA_KDIM), F32),
        pltpu.VMEM((nb, cg, GLA_KDIM), F32),
        pltpu.VMEM((nb, cg, GLA_KDIM), F32),
    ]
    if direction == 1:
        scratch.append(pltpu.VMEM((nb, cg, GLA_W), F32))
    return pl.pallas_call(
        kern,
        grid=(nch,),
        in_specs=specs,
        out_specs=blk(GLA_W),
        out_shape=jax.ShapeDtypeStruct((nb, rb, GLA_W), F32 if direction == 0 else BF16),
        scratch_shapes=scratch,
        compiler_params=_scan_params(),
        name="gla_fwd" if direction == 0 else "gla_bwd",
    )(*ins)


def _ssd_kernel(*refs, direction, nb, c):
    if direction == 0:
        (xc_ref, sm_ref, dtb_ref, alog_ref, tri_ref, x_ref, hm_ref, o_ref, st_ref) = refs
    else:
        (xc_ref, sm_ref, z_ref, of_ref, dtb_ref, alog_ref, tri_ref, x_ref, hm_ref, dsk_ref, nw_ref,
         o_ref, st_ref) = refs
    hpg = SSD_HEADS // SSD_GROUPS
    gw = hpg * SSD_HEAD_DIM

    @pl.when(pl.program_id(0) == 0)
    def _():
        st_ref[...] = jnp.zeros_like(st_ref)

    ii = lax.broadcasted_iota(jnp.int32, (c, c), 0)
    jj = lax.broadcasted_iota(jnp.int32, (c, c), 1)
    tri_mask = (ii >= jj) if direction == 0 else (ii <= jj)
    neg_a = -jnp.exp(alog_ref[...])

    for b in range(nb):
        raw = sm_ref[b] + dtb_ref[...]
        dt = jnp.maximum(raw, 0.0) + jnp.log1p(jnp.exp(-jnp.abs(raw)))
        g = _dot_sel_rhs(tri_ref[...], dt * neg_a)
        gt = g.T
        g_last = g[c - 1:c, :] if direction == 0 else g[0:1, :]
        stack = jnp.concatenate(
            [jnp.exp(g), dt, dt * jnp.exp(g_last - g), jnp.broadcast_to(jnp.exp(g_last), (8, LANES))], axis=0)
        ex = _dot_sel_lhs(stack, x_ref[...])
        e_in, dt_x, w_st, gl_x = ex[0:c], ex[c:2 * c], ex[2 * c:3 * c], ex[3 * c:3 * c + 1]

        xs = xc_ref[b, :, 0:SSD_W]
        xw = xs * dt_x
        xst = (xs * w_st).astype(BF16)
        outs = []
        for grp in range(SSD_GROUPS):
            b0 = SSD_W + grp * SSD_STATE
            c0 = SSD_W + SSD_GROUPS * SSD_STATE + grp * SSD_STATE
            bm = xc_ref[b, :, b0:b0 + SSD_STATE].astype(BF16)
            cm = xc_ref[b, :, c0:c0 + SSD_STATE].astype(BF16)
            cb = _dot_nt(cm, bm)
            xw_g = xw[:, grp * gw:(grp + 1) * gw]
            a_parts, x_parts = [], []
            for hh in range(hpg):
                lane = SMALL_DT_LANE + grp * hpg + hh
                dec = jnp.where(tri_mask,
                                jnp.exp(jnp.minimum(g[:, lane:lane + 1] - gt[lane:lane + 1, :], 0.0)), 0.0)
                a_parts.append((cb * dec).astype(BF16))
                x_parts.append((xw_g * hm_ref[hh:hh + 1, :]).astype(BF16))
            intra = _dot(jnp.concatenate(a_parts, axis=1), jnp.concatenate(x_parts, axis=0))
            st = st_ref[b, grp]
            inter = e_in[:, grp * gw:(grp + 1) * gw] * _dot(cm, st.astype(BF16))
            outs.append(intra + inter)
            st_ref[b, grp] = (st * gl_x[:, grp * gw:(grp + 1) * gw]
                              + _dot_tn(bm, xst[:, grp * gw:(grp + 1) * gw]))
        o = jnp.concatenate(outs, axis=1)
        if direction == 0:
            o_ref[b] = o
        else:
            y = (of_ref[b] + o + dsk_ref[...] * xs) * _silu(z_ref[b])
            o_ref[b] = _rms(y, nw_ref[...]).astype(BF16)


def _ssd_call(direction, xconv, small_d, p_z, o_f, dtb_d, alog_d, dsk, nw, consts, *, nctx):
    nb, rb, _ = xconv.shape
    c = SSD_CHUNK
    nch, ncc = rb // c, nctx // c
    kern = functools.partial(_ssd_kernel, direction=direction, nb=nb, c=c)
    chunk = lambda n: (0, _chunk_of(direction, n, ncc, nch), 0)
    const2 = lambda n: (0, 0)
    blk = lambda w: pl.BlockSpec((nb, c, w), chunk)
    full = lambda a: pl.BlockSpec(a.shape, const2)
    ins = [xconv, small_d]
    specs = [blk(SSD_CONV_CH), blk(LANES)]
    if direction == 1:
        ins += [p_z, o_f]
        specs += [blk(SSD_W), blk(SSD_W)]
    tail = [dtb_d, alog_d, consts["tri_ssd"][direction], consts["ssd_x"], consts["hmask"]]
    if direction == 1:
        tail += [dsk, nw]
    ins += tail
    specs += [full(a) for a in tail]
    return pl.pallas_call(
        kern,
        grid=(nch,),
        in_specs=specs,
        out_specs=blk(SSD_W),
        out_shape=jax.ShapeDtypeStruct((nb, rb, SSD_W), F32 if direction == 0 else BF16),
        scratch_shapes=[pltpu.VMEM((nb, SSD_GROUPS, SSD_STATE, 256), F32)],
        compiler_params=_scan_params(),
        name="ssd_fwd" if direction == 0 else "ssd_bwd",
    )(*ins)


def _ret_log_gamma_row(width):
    lane = lax.broadcasted_iota(jnp.int32, (1, width), 1)
    row = jnp.zeros((1, width), F32)
    for h in range(RET_HEADS):
        lg = math.log1p(-(2.0 ** (-5.0 - h)))
        row = jnp.where(lane // RET_HEAD_DIM == h, lg, row)
    return row


def _rope(t, cos, sin_signed):
    lane = lax.broadcasted_iota(jnp.int32, t.shape, 1)
    half = RET_HEAD_DIM // 2
    swapped = jnp.where(lane % RET_HEAD_DIM < half,
                        pltpu.roll(t, LANES - half, 1), pltpu.roll(t, half, 1))
    return t * cos + swapped * sin_signed


def _ret_kernel(*refs, direction, nb, c):
    if direction == 0:
        (p_ref, cos_ref, sin_ref, avg_ref, hm_ref, o_ref, st_ref) = refs
    else:
        (p_ref, cos_ref, sin_ref, of_ref, avg_ref, hm_ref, nw_ref, o_ref, st_ref) = refs
    w = RET_W

    @pl.when(pl.program_id(0) == 0)
    def _():
        st_ref[...] = jnp.zeros_like(st_ref)

    cos = cos_ref[...]
    sin = sin_ref[...]
    lg_row = _ret_log_gamma_row(w)
    pos = lax.broadcasted_iota(jnp.int32, (c, 1), 0)
    cnt_in = (pos + 1 if direction == 0 else c - pos).astype(F32)
    cnt_st = (c - 1 - pos if direction == 0 else pos).astype(F32)
    e_in = jnp.exp(cnt_in * lg_row)
    e_st = jnp.exp(cnt_st * lg_row)
    e_all = jnp.exp(float(c) * lg_row)
    ii = lax.broadcasted_iota(jnp.int32, (c, c), 0)
    jj = lax.broadcasted_iota(jnp.int32, (c, c), 1)
    dist = (ii - jj) if direction == 0 else (jj - ii)
    dec_parts = []
    for h in range(RET_HEADS):
        lg = math.log1p(-(2.0 ** (-5.0 - h)))
        dec_parts.append(jnp.where(dist >= 0, jnp.exp(dist.astype(F32) * lg), 0.0))
    dec = jnp.concatenate(dec_parts, axis=1)
    bd_mask = avg_ref[...].astype(F32)

    for b in range(nb):
        q_halves, k_halves = [], []
        for hf in range(w // LANES):
            qh = p_ref[b, :, hf * LANES:(hf + 1) * LANES] * (RET_HEAD_DIM ** -0.5)
            kh = p_ref[b, :, w + hf * LANES:w + (hf + 1) * LANES]
            q_halves.append(_rope(qh, cos, sin))
            k_halves.append(_rope(kh, cos, sin))
        q = jnp.concatenate(q_halves, axis=1)
        k = jnp.concatenate(k_halves, axis=1)
        v = p_ref[b, :, 2 * w:3 * w]
        qb = q.astype(BF16)
        k_parts, v_parts = [], []
        for h in range(RET_HEADS):
            hm = hm_ref[h:h + 1, :]
            k_parts.append((k * hm).astype(BF16))
            v_parts.append((v * hm).astype(BF16))
        k_bd = jnp.concatenate(k_parts, axis=0)
        v_bd = jnp.concatenate(v_parts, axis=0)
        scores = _dot_nt(qb, k_bd) * dec
        intra = _dot(scores.astype(BF16), v_bd)
        st = st_ref[b]
        o = intra + e_in * _dot(qb, st.astype(BF16))
        st_ref[b] = st * e_all + _dot_tn((k * e_st).astype(BF16), v.astype(BF16)) * bd_mask
        if direction == 0:
            o_ref[b] = o
        else:
            tot = of_ref[b] + o
            mu = _dot_sel_lhs(tot, avg_ref[...], 2) * (1.0 / RET_HEAD_DIM)
            xc = tot - mu
            var = _dot_sel_lhs(xc * xc, avg_ref[...], 2) * (1.0 / RET_HEAD_DIM)
            y = xc * lax.rsqrt(var + RMS_EPS) * nw_ref[...]
            o_ref[b] = (y * _silu(p_ref[b, :, 3 * w:4 * w])).astype(BF16)


def _ret_call(direction, p_ret, cos_t, sin_t, o_f, nw, consts, *, nctx):
    nb, rb, _ = p_ret.shape
    c = RET_CHUNK
    nch, ncc = rb // c, nctx // c
    kern = functools.partial(_ret_kernel, direction=direction, nb=nb, c=c)
    chunk = lambda n: (0, _chunk_of(direction, n, ncc, nch), 0)
    tchunk = lambda n: (_chunk_of(direction, n, ncc, nch), 0)
    const2 = lambda n: (0, 0)
    blk = lambda w: pl.BlockSpec((nb, c, w), chunk)
    full = lambda a: pl.BlockSpec(a.shape, const2)
    ins = [p_ret, cos_t, sin_t]
    specs = [blk(4 * RET_W), pl.BlockSpec((c, LANES), tchunk), pl.BlockSpec((c, LANES), tchunk)]
    if direction == 1:
        ins.append(o_f)
        specs.append(blk(RET_W))
    tail = [consts["avg256"], consts["hmask"]]
    if direction == 1:
        tail.append(nw)
    ins += tail
    specs += [full(a) for a in tail]
    return pl.pallas_call(
        kern,
        grid=(nch,),
        in_specs=specs,
        out_specs=blk(RET_W),
        out_shape=jax.ShapeDtypeStruct((nb, rb, RET_W), F32 if direction == 0 else BF16),
        scratch_shapes=[pltpu.VMEM((nb, RET_W, RET_W), F32)],
        compiler_params=_scan_params(),
        name="ret_fwd" if direction == 0 else "ret_bwd",
    )(*ins)


def _outproj_kernel(h_ref, mod_ref, gla_ref, ssd_ref, ret_ref, wo_ref, nw_ref, o_ref, *, tm, spb, ncs, nb):
    i = pl.program_id(0)
    d = D_MODEL
    mixed = jnp.concatenate([gla_ref[...], ssd_ref[...], ret_ref[...]], axis=1)
    m = _dot(mixed, wo_ref[...])
    for k in range(tm // SLAB):
        mrow = _mod_row(i * (tm // SLAB) + k, spb, ncs, nb)
        gate = mod_ref[mrow][:, 2 * d:3 * d]
        sl = slice(k * SLAB, (k + 1) * SLAB)
        o_ref[sl, :] = h_ref[sl, :] + gate * _rms(m[sl, :], nw_ref[...])


def _outproj_call(h, mod, gla, ssd, ret, wo, nw, *, spb, ncs, nb):
    r, d = h.shape
    tm = TM_DENSE
    kern = functools.partial(_outproj_kernel, tm=tm, spb=spb, ncs=ncs, nb=nb)
    row = lambda i: (i, 0)
    const = lambda i: (0, 0)
    return pl.pallas_call(
        kern,
        grid=(r // tm,),
        in_specs=[
            pl.BlockSpec((tm, d), row),
            pl.BlockSpec(mod.shape, lambda i: (0, 0, 0)),
            pl.BlockSpec((tm, GLA_W), row),
            pl.BlockSpec((tm, SSD_W), row),
            pl.BlockSpec((tm, RET_W), row),
            pl.BlockSpec(wo.shape, const),
            pl.BlockSpec((1, d), const),
        ],
        out_specs=pl.BlockSpec((tm, d), row),
        out_shape=jax.ShapeDtypeStruct((r, d), F32),
        compiler_params=pltpu.CompilerParams(
            dimension_semantics=("arbitrary",), vmem_limit_bytes=VMEM_LIMIT),
        name="out_proj",
    )(h, mod, gla, ssd, ret, wo, nw)


def _ffn_kernel(h_ref, mod_ref, npre_ref, w13_ref, w2_ref, npost_ref, o_ref, hb_ref, acc_ref,
                *, tm, spb, ncs, nb):
    i = pl.program_id(0)
    d = D_MODEL
    for k in range(tm // SLAB):
        mrow = _mod_row(i * (tm // SLAB) + k, spb, ncs, nb)
        shift = mod_ref[mrow][:, 3 * d:4 * d]
        scale = mod_ref[mrow][:, 4 * d:5 * d]
        sl = slice(k * SLAB, (k + 1) * SLAB)
        y = _rms(h_ref[sl, :], npre_ref[...]) * (1.0 + scale) + shift
        hb_ref[sl, :] = y.astype(BF16)
    hb = hb_ref[...]
    for cidx in range(FFN_HIDDEN // FFN_CHUNK):
        lo = cidx * FFN_CHUNK
        gate = _dot(hb, w13_ref[:, lo:lo + FFN_CHUNK])
        up = _dot(hb, w13_ref[:, FFN_HIDDEN + lo:FFN_HIDDEN + lo + FFN_CHUNK])
        act = (_silu(gate) * up).astype(BF16)
        part = _dot(act, w2_ref[lo:lo + FFN_CHUNK, :])
        if cidx == 0:
            acc_ref[...] = part
        else:
            acc_ref[...] += part
    for k in range(tm // SLAB):
        mrow = _mod_row(i * (tm // SLAB) + k, spb, ncs, nb)
        gate = mod_ref[mrow][:, 5 * d:6 * d]
        sl = slice(k * SLAB, (k + 1) * SLAB)
        o_ref[sl, :] = h_ref[sl, :] + gate * _rms(acc_ref[sl, :], npost_ref[...])


def _ffn_call(h, mod, npre, w13, w2, npost, *, spb, ncs, nb):
    r, d = h.shape
    tm = TM_DENSE
    kern = functools.partial(_ffn_kernel, tm=tm, spb=spb, ncs=ncs, nb=nb)
    row = lambda i: (i, 0)
    const = lambda i: (0, 0)
    return pl.pallas_call(
        kern,
        grid=(r // tm,),
        in_specs=[
            pl.BlockSpec((tm, d), row),
            pl.BlockSpec(mod.shape, lambda i: (0, 0, 0)),
            pl.BlockSpec((1, d), const),
            pl.BlockSpec(w13.shape, const),
            pl.BlockSpec(w2.shape, const),
            pl.BlockSpec((1, d), const),
        ],
        out_specs=pl.BlockSpec((tm, d), row),
        out_shape=jax.ShapeDtypeStruct((r, d), F32),
        scratch_shapes=[pltpu.VMEM((tm, d), BF16), pltpu.VMEM((tm, d), F32)],
        compiler_params=pltpu.CompilerParams(
            dimension_semantics=("arbitrary",), vmem_limit_bytes=VMEM_LIMIT),
        name="ffn",
    )(h, mod, npre, w13, w2, npost)


def _tri_pair(c):
    i = jnp.arange(c)[:, None]
    j = jnp.arange(c)[None, :]
    return jnp.stack([(j <= i), (j >= i)]).astype(BF16)


def _gla_cum_pair(cg, sb):
    i = jnp.arange(cg)[:, None]
    j = jnp.arange(cg)[None, :]
    same = (i // sb) == (j // sb)
    fwd = jnp.concatenate([same & (j <= i), same], axis=0)
    bwd = jnp.concatenate([same & (j >= i), same], axis=0)
    return jnp.stack([fwd, bwd]).astype(BF16)


def _constants():
    lane256 = jnp.arange(256)
    consts = {
        "gla_cum": _gla_cum_pair(GLA_BLOCK, GLA_SUB),
        "tri_ssd": _tri_pair(SSD_CHUNK),
        "avg256": (lane256[:, None] // 64 == lane256[None, :] // 64).astype(BF16),
        "gla_e2": (jnp.arange(GLA_KDIM)[:, None] // GLA_HEAD_K == lane256[None, :] // GLA_HEAD_V).astype(BF16),
        "gla_bdm": (lane256[:, None] // GLA_HEAD_V == jnp.arange(GLA_KDIM)[None, :] // GLA_HEAD_K).astype(F32),
        "ssd_x": (jnp.arange(LANES)[:, None] - SMALL_DT_LANE
                  == jnp.arange(SSD_W)[None, :] // SSD_HEAD_DIM).astype(BF16),
        "hmask": (jnp.arange(8)[:, None] == lane256[None, :] // 64).astype(F32),
    }
    return consts


def _rope_tables(t_lat, nctx):
    rows = t_lat // GRID_W
    row = jnp.repeat(jnp.arange(rows), GRID_W).astype(F32)
    col = jnp.tile(jnp.arange(GRID_W), rows).astype(F32)
    n_freq = RET_HEAD_DIM // 4
    inv_freq = ROPE_BASE ** (-jnp.arange(n_freq, dtype=F32) / n_freq)
    ang = jnp.concatenate([row[:, None] * inv_freq, col[:, None] * inv_freq], axis=-1)
    cos = jnp.cos(ang)
    sin = jnp.sin(ang)
    cos_t = jnp.concatenate([cos, cos, cos, cos], axis=-1)
    sin_t = jnp.concatenate([-sin, sin, -sin, sin], axis=-1)
    cos_t = jnp.concatenate([jnp.ones((nctx, LANES), F32), cos_t], axis=0)
    sin_t = jnp.concatenate([jnp.zeros((nctx, LANES), F32), sin_t], axis=0)
    return cos_t, sin_t


def kernel(x, c, ctx, c_ctx, ada_w, ada_b, norm_mix_pre, norm_mix_post, norm_ffn_pre, norm_ffn_post,
           w_in, w_out, gla_gate_up, gla_gate_b, gla_norm, ssd_conv_w, ssd_conv_b, ssd_dt_bias,
           ssd_a_log, ssd_d, ssd_norm, ret_norm, ffn_w13, ffn_w2):
    nb, t_lat, d = x.shape
    nctx = ctx.shape[1]
    depth = ada_w.shape[0]
    rb = nctx + t_lat
    r = nb * rb
    spb = rb // SLAB
    ncs = nctx // SLAB
    assert d == D_MODEL and nb + 1 <= 8
    assert nctx % SLAB == 0 and t_lat % SLAB == 0 and r % TM_DENSE == 0

    consts = _constants()
    cos_t, sin_t = _rope_tables(t_lat, nctx)

    cvec = jnp.zeros((8, d), F32).at[:nb].set(c).at[nb].set(c_ctx)
    mods = _ada_call(cvec, ada_w, ada_b)

    s1, s2 = GLA_COLS, GLA_COLS + SSD_COLS
    lr0 = 2 * GLA_KDIM + 2 * GLA_W
    dt0 = s1 + SSD_W + SSD_CONV_CH
    wm = jnp.concatenate([w_in[:, :, 0:lr0], w_in[:, :, s1:dt0], w_in[:, :, s2:]], axis=-1).astype(BF16)
    zpad = jnp.zeros((depth, d, LANES - GLA_RANK - SSD_HEADS), F32)
    ws = jnp.concatenate(
        [w_in[:, :, lr0:lr0 + GLA_RANK], w_in[:, :, dt0:dt0 + SSD_HEADS], zpad,
         w_in[:, :, lr0 + GLA_RANK:lr0 + 2 * GLA_RANK], w_in[:, :, dt0 + SSD_HEADS:dt0 + 2 * SSD_HEADS], zpad],
        axis=-1).astype(BF16)
    wo = w_out.astype(BF16)
    w13 = ffn_w13.astype(BF16)
    w2 = ffn_w2.astype(BF16)

    wg = jnp.zeros((depth, 2, LANES, GLA_KDIM), F32).at[:, :, :GLA_RANK, :].set(gla_gate_up)
    gb = gla_gate_b.reshape(depth, 2, 1, GLA_KDIM)
    lane_dt = slice(SMALL_DT_LANE, SMALL_DT_LANE + SSD_HEADS)
    dtb = jnp.zeros((depth, 2, 1, LANES), F32).at[:, :, 0, lane_dt].set(ssd_dt_bias)
    alog = jnp.zeros((depth, 2, 1, LANES), F32).at[:, :, 0, lane_dt].set(ssd_a_log)
    dsk = jnp.repeat(ssd_d, SSD_HEAD_DIM, axis=-1).reshape(depth, 1, SSD_W)
    conv_w = jnp.zeros((depth, 8, SSD_CONV_CH), F32).at[:, :SSD_CONV_W].set(ssd_conv_w)

    h = jnp.concatenate([ctx, x], axis=1).reshape(r, d)
    meta = dict(spb=spb, ncs=ncs, nb=nb)
    row1 = lambda a: a.reshape(1, -1)
    for l in range(depth):
        mod = mods[l].reshape(8, 1, 6 * d)
        p_gla, p_z, p_xbc, p_ret, small = _inproj_call(h, mod, row1(norm_mix_pre[l]), wm[l], ws[l], **meta)
        small = small.reshape(2, nb, rb, LANES)
        xconv = _conv_call(p_xbc.reshape(nb, rb, SSD_CONV_CH), conv_w[l], row1(ssd_conv_b[l]), ncs=ncs)
        p_gla = p_gla.reshape(nb, rb, 768)
        p_z = p_z.reshape(nb, rb, SSD_W)
        p_ret = p_ret.reshape(nb, rb, 4 * RET_W)
        gla_f = _gla_call(0, p_gla, small[0], None, wg[l, 0], gb[l, 0], None, consts, nctx=nctx)
        gla = _gla_call(1, p_gla, small[1], gla_f, wg[l, 1], gb[l, 1], row1(gla_norm[l]), consts, nctx=nctx)
        ssd_f = _ssd_call(0, xconv, small[0], None, None, dtb[l, 0], alog[l, 0], None, None, consts, nctx=nctx)
        ssd = _ssd_call(1, xconv, small[1], p_z, ssd_f, dtb[l, 1], alog[l, 1], dsk[l], row1(ssd_norm[l]),
                        consts, nctx=nctx)
        ret_f = _ret_call(0, p_ret, cos_t, sin_t, None, None, consts, nctx=nctx)
        ret = _ret_call(1, p_ret, cos_t, sin_t, ret_f, row1(ret_norm[l]), consts, nctx=nctx)
        h = _outproj_call(h, mod, gla.reshape(r, GLA_W), ssd.reshape(r, SSD_W), ret.reshape(r, RET_W),
                          wo[l], row1(norm_mix_post[l]), **meta)
        h = _ffn_call(h, mod, row1(norm_ffn_pre[l]), w13[l], w2[l], row1(norm_ffn_post[l]), **meta)
    return h.reshape(nb, rb, d)[:, nctx:, :]
```

```python
import functools
import math

import jax
import jax.numpy as jnp
from jax import lax
from jax.experimental import pallas as pl
from jax.experimental.pallas import tpu as pltpu

F32 = jnp.float32
BF16 = jnp.bfloat16

D_MODEL = 1024
GRID_W = 64
RMS_EPS = 1e-6
GLA_W = 256
GLA_HEADS = 4
GLA_HEAD_V = 64
GLA_HEAD_K = 32
GLA_KDIM = 128
GLA_RANK = 16
GLA_GATE_TAU = 16.0
SSD_W = 512
SSD_HEADS = 8
SSD_HEAD_DIM = 64
SSD_GROUPS = 2
SSD_STATE = 128
SSD_CONV_W = 5
SSD_CONV_CH = 1024
RET_W = 256
RET_HEADS = 4
RET_HEAD_DIM = 64
ROPE_BASE = 10000.0
GLA_COLS = 800
SSD_COLS = 1552
FFN_HIDDEN = 2816

LANES = 128
SLAB = 256
TM_DENSE = 512
FFN_CHUNK = 256
GLA_SUB = 16
GLA_BLOCK = 128
SSD_CHUNK = 128
RET_CHUNK = 128
SMALL_DT_LANE = 16
VMEM_LIMIT = 56 * 1024 * 1024


def _silu(x):
    return x * jax.nn.sigmoid(x)


def _dot(a, b):
    return jnp.dot(a, b, preferred_element_type=F32)


def _dot_nt(a, b):
    return lax.dot_general(a, b, (((1,), (1,)), ((), ())), preferred_element_type=F32)


def _dot_tn(a, b):
    return lax.dot_general(a, b, (((0,), (0,)), ((), ())), preferred_element_type=F32)


def _split_bf16(x, terms):
    parts = []
    r = x
    for _ in range(terms):
        hi = r.astype(BF16)
        parts.append(hi)
        r = r - hi.astype(F32)
    return parts


def _dot_sel_rhs(m01, x, terms=3):
    acc = None
    for p in _split_bf16(x, terms):
        t = _dot(m01, p)
        acc = t if acc is None else acc + t
    return acc


def _dot_sel_lhs(x, m01, terms=3):
    acc = None
    for p in _split_bf16(x, terms):
        t = _dot(p, m01)
        acc = t if acc is None else acc + t
    return acc


def _dot3(a, b):
    a_hi = a.astype(BF16)
    a_lo = (a - a_hi.astype(F32)).astype(BF16)
    b_hi = b.astype(BF16)
    b_lo = (b - b_hi.astype(F32)).astype(BF16)
    return _dot(a_hi, b_hi) + _dot(a_lo, b_hi) + _dot(a_hi, b_lo)


def _mod_row(slab, spb, ncs, nb):
    return jnp.where(slab % spb < ncs, nb, slab // spb)


def _rms(x, w):
    return x * lax.rsqrt(jnp.mean(x * x, axis=-1, keepdims=True) + RMS_EPS) * w


def _ada_kernel(cv_ref, w_ref, b_ref, o_ref):
    s = _silu(cv_ref[...]).astype(BF16)
    o_ref[...] = _dot(s, w_ref[...].astype(BF16)) + b_ref[...]


def _ada_call(cvec, ada_w, ada_b):
    depth, d, n6 = ada_w.shape
    tn = 2048
    return pl.pallas_call(
        _ada_kernel,
        grid=(depth, n6 // tn),
        in_specs=[
            pl.BlockSpec((8, d), lambda l, j: (0, 0)),
            pl.BlockSpec((None, d, tn), lambda l, j: (l, 0, j)),
            pl.BlockSpec((None, 1, tn), lambda l, j: (l, 0, j)),
        ],
        out_specs=pl.BlockSpec((None, 8, tn), lambda l, j: (l, 0, j)),
        out_shape=jax.ShapeDtypeStruct((depth, 8, n6), F32),
        compiler_params=pltpu.CompilerParams(
            dimension_semantics=("arbitrary", "arbitrary"), vmem_limit_bytes=VMEM_LIMIT),
        name="ada_mod",
    )(cvec, ada_w, ada_b.reshape(depth, 1, n6))


def _inproj_kernel(h_ref, hp_ref, hn_ref, mod_ref, nw_ref, wm_ref, ws_ref, cw_ref, cb_ref,
                   gla_ref, z_ref, xc_ref, ret_ref, small_ref, yb_ref, scr_ref, *, tm, spb, ncs, nb):
    i = pl.program_id(0)
    d = D_MODEL
    nsl = tm // SLAB
    x0, x1 = 1280, 2304

    def normed(x, mrow):
        return _rms(x, nw_ref[...]) * (1.0 + mod_ref[mrow][:, d:2 * d]) + mod_ref[mrow][:, 0:d]

    mrows = [_mod_row(i * nsl + k, spb, ncs, nb) for k in range(nsl)]
    for k in range(nsl):
        yb_ref[k * SLAB:(k + 1) * SLAB, :] = normed(h_ref[k * SLAB:(k + 1) * SLAB, :], mrows[k]).astype(BF16)
    yb = yb_ref[...]
    pm = _dot(yb, wm_ref[...])
    gla_ref[...] = pm[:, 0:768]
    z_ref[...] = pm[:, 768:x0]
    ret_ref[...] = pm[:, x1:3328]
    xbc = pm[:, x0:x1]
    ps = _dot(yb, ws_ref[...])
    small_ref[0] = ps[:, 0:LANES]
    small_ref[1] = ps[:, LANES:2 * LANES]

    yh = jnp.concatenate([normed(hp_ref[...], mrows[0]), normed(hn_ref[...], mrows[nsl - 1])], axis=0)
    halo = _dot(yh.astype(BF16), wm_ref[:, x0:x1])
    pad = (SSD_CONV_W - 1) // 2
    for k in range(nsl):
        pos = (i * nsl + k) % spb
        first = jnp.logical_or(pos == 0, pos == ncs)
        last = jnp.logical_or(pos == ncs - 1, pos == spb - 1)
        before = halo[0:8] if k == 0 else xbc[k * SLAB - 8:k * SLAB, :]
        after = halo[8:16] if k == nsl - 1 else xbc[(k + 1) * SLAB:(k + 1) * SLAB + 8, :]
        scr_ref[0:8, :] = jnp.where(first, 0.0, before)
        scr_ref[8:8 + SLAB, :] = xbc[k * SLAB:(k + 1) * SLAB, :]
        scr_ref[8 + SLAB:16 + SLAB, :] = jnp.where(last, 0.0, after)
        acc = cb_ref[...] + cw_ref[0:1, :] * scr_ref[pl.ds(8 - pad, SLAB), :]
        for t in range(1, SSD_CONV_W):
            acc = acc + cw_ref[t:t + 1, :] * scr_ref[pl.ds(8 - pad + t, SLAB), :]
        xc_ref[k * SLAB:(k + 1) * SLAB, :] = _silu(acc)


def _inproj_call(h, mod, nw, wm, ws, cw, cb, *, spb, ncs, nb):
    r, d = h.shape
    tm = TM_DENSE
    kern = functools.partial(_inproj_kernel, tm=tm, spb=spb, ncs=ncs, nb=nb)
    row = lambda i: (i, 0)
    const = lambda i: (0, 0)
    per = tm // 8
    return pl.pallas_call(
        kern,
        grid=(r // tm,),
        in_specs=[
            pl.BlockSpec((tm, d), row),
            pl.BlockSpec((8, d), lambda i: (jnp.maximum(i * per - 1, 0), 0)),
            pl.BlockSpec((8, d), lambda i: (jnp.minimum((i + 1) * per, r // 8 - 1), 0)),
            pl.BlockSpec(mod.shape, lambda i: (0, 0, 0)),
            pl.BlockSpec((1, d), const),
            pl.BlockSpec(wm.shape, const),
            pl.BlockSpec(ws.shape, const),
            pl.BlockSpec(cw.shape, const),
            pl.BlockSpec(cb.shape, const),
        ],
        out_specs=[
            pl.BlockSpec((tm, 768), row),
            pl.BlockSpec((tm, 512), row),
            pl.BlockSpec((tm, 1024), row),
            pl.BlockSpec((tm, 1024), row),
            pl.BlockSpec((2, tm, LANES), lambda i: (0, i, 0)),
        ],
        out_shape=[
            jax.ShapeDtypeStruct((r, 768), F32),
            jax.ShapeDtypeStruct((r, 512), F32),
            jax.ShapeDtypeStruct((r, 1024), F32),
            jax.ShapeDtypeStruct((r, 1024), F32),
            jax.ShapeDtypeStruct((2, r, LANES), F32),
        ],
        scratch_shapes=[pltpu.VMEM((tm, d), BF16), pltpu.VMEM((SLAB + 16, SSD_CONV_CH), F32)],
        compiler_params=pltpu.CompilerParams(
            dimension_semantics=("arbitrary",), vmem_limit_bytes=VMEM_LIMIT),
        name="in_proj",
    )(h, h, h, mod, nw, wm, ws, cw, cb)


def _chunk_of(direction, n, ncc, nch):
    if direction == 0:
        return n
    return jnp.where(n < ncc, ncc - 1 - n, nch + ncc - 1 - n)


def _scan_params():
    return pltpu.CompilerParams(dimension_semantics=("arbitrary",), vmem_limit_bytes=VMEM_LIMIT)


def _gla_kernel(*refs, direction, nb, cg):
    if direction == 0:
        (p_ref, sm_ref, wg_ref, gb_ref, cum_ref, e2_ref, bdm_ref,
         o_ref, st_ref, qs_ref, g_ref, kk_ref, egl_ref) = refs
        dst_ref = o_ref
    else:
        (p_ref, sm_ref, of_ref, wg_ref, gb_ref, cum_ref, e2_ref, bdm_ref, avg_ref, nw_ref,
         o_ref, st_ref, qs_ref, g_ref, kk_ref, egl_ref, oc_ref) = refs
        dst_ref = oc_ref
    sb = GLA_SUB
    nsb = cg // sb

    @pl.when(pl.program_id(0) == 0)
    def _():
        st_ref[...] = jnp.zeros_like(st_ref)

    for b in range(nb):
        z = _dot3(sm_ref[b], wg_ref[...]) + gb_ref[...]
        logg = (jnp.minimum(z, 0.0) - jnp.log1p(jnp.exp(-jnp.abs(z)))) * (1.0 / GLA_GATE_TAU)
        cums = _dot_sel_rhs(cum_ref[...], logg)
        gs = cums[0:cg]
        gl = cums[cg:2 * cg]
        g_ref[b] = gs
        egl_ref[b] = jnp.exp(gl)
        qs_ref[b] = p_ref[b, :, 0:GLA_KDIM] * (GLA_HEAD_K ** -0.5)
        kk_ref[b] = p_ref[b, :, GLA_KDIM:2 * GLA_KDIM] * jnp.exp(gl - gs)

    rows16 = lax.broadcasted_iota(jnp.int32, (sb, GLA_KDIM), 0)
    masks = [(rows16 >= j) if direction == 0 else (rows16 <= j) for j in range(sb)]

    def body(it, carry):
        a = it if direction == 0 else nsb - 1 - it
        rs = pl.ds(pl.multiple_of(a * sb, sb), sb)
        for b in range(nb):
            gs = g_ref[b, rs, :]
            qs = qs_ref[b, rs, :]
            kb = p_ref[b, rs, GLA_KDIM:2 * GLA_KDIM]
            vb = p_ref[b, rs, 2 * GLA_KDIM:2 * GLA_KDIM + GLA_W]
            st = st_ref[b]
            inter = _dot_nt((qs * jnp.exp(gs)).astype(BF16), st.astype(BF16))
            parts = []
            for j in range(sb):
                dec = jnp.where(masks[j], jnp.exp(gs - gs[j:j + 1, :]), 0.0)
                parts.append((dec * qs * kb[j:j + 1, :]).astype(BF16))
            sx = _dot(jnp.concatenate(parts, axis=0), e2_ref[...])
            intra = sx[0:sb] * vb[0:1, :]
            for j in range(1, sb):
                intra = intra + sx[j * sb:(j + 1) * sb] * vb[j:j + 1, :]
            dst_ref[b, rs, :] = intra + inter
            upd = _dot_tn(vb.astype(BF16), kk_ref[b, rs, :].astype(BF16))
            st_ref[b] = st * egl_ref[b, rs, :][0:1, :] + upd * bdm_ref[...]
        return carry

    lax.fori_loop(0, nsb, body, 0)

    if direction == 1:
        for b in range(nb):
            tot = of_ref[b] + oc_ref[b]
            ms = _dot_sel_lhs(tot * tot, avg_ref[...], 2) * (1.0 / GLA_HEAD_V)
            y = tot * lax.rsqrt(ms + RMS_EPS) * nw_ref[...]
            r = p_ref[b, :, 2 * GLA_KDIM + GLA_W:2 * GLA_KDIM + 2 * GLA_W]
            o_ref[b] = (y * _silu(r)).astype(BF16)


def _gla_call(direction, p_gla, small_d, o_f, wg_d, gb_d, nw, consts, *, nctx):
    nb, rb, _ = p_gla.shape
    cg = GLA_BLOCK
    nch, ncc = rb // cg, nctx // cg
    kern = functools.partial(_gla_kernel, direction=direction, nb=nb, cg=cg)
    chunk = lambda n: (0, _chunk_of(direction, n, ncc, nch), 0)
    const2 = lambda n: (0, 0)
    blk = lambda w: pl.BlockSpec((nb, cg, w), chunk)
    full = lambda a: pl.BlockSpec(a.shape, const2)
    cum = consts["gla_cum"][direction]
    ins = [p_gla, small_d]
    specs = [blk(768), blk(LANES)]
    if direction == 1:
        ins.append(o_f)
        specs.append(blk(GLA_W))
    tail = [wg_d, gb_d, cum, consts["gla_e2"], consts["gla_bdm"]]
    if direction == 1:
        tail += [consts["avg256"], nw]
    ins += tail
    specs += [full(a) for a in tail]
    scratch = [
        pltpu.VMEM((nb, GLA_W, GLA_KDIM), F32),
        pltpu.VMEM((nb, cg, GLA_KDIM), F32),
        pltpu.VMEM((nb, cg, GLA_KDIM), F32),
        pltpu.VMEM((nb, cg, GLA_KDIM), F32),
        pltpu.VMEM((nb, cg, GLA_KDIM), F32),
    ]
    if direction == 1:
        scratch.append(pltpu.VMEM((nb, cg, GLA_W), F32))
    return pl.pallas_call(
        kern,
        grid=(nch,),
        in_specs=specs,
        out_specs=blk(GLA_W),
        out_shape=jax.ShapeDtypeStruct((nb, rb, GLA_W), F32 if direction == 0 else BF16),
        scratch_shapes=scratch,
        compiler_params=_scan_params(),
        name="gla_fwd" if direction == 0 else "gla_bwd",
    )(*ins)


def _ssd_kernel(*refs, direction, nb, c):
    if direction == 0:
        (xc_ref, sm_ref, dtb_ref, alog_ref, tri_ref, x_ref, hm_ref, o_ref, st_ref) = refs
    else:
        (xc_ref, sm_ref, z_ref, of_ref, dtb_ref, alog_ref, tri_ref, x_ref, hm_ref, dsk_ref, nw_ref,
         o_ref, st_ref) = refs
    hpg = SSD_HEADS // SSD_GROUPS
    gw = hpg * SSD_HEAD_DIM

    @pl.when(pl.program_id(0) == 0)
    def _():
        st_ref[...] = jnp.zeros_like(st_ref)

    ii = lax.broadcasted_iota(jnp.int32, (c, c), 0)
    jj = lax.broadcasted_iota(jnp.int32, (c, c), 1)
    tri_mask = (ii >= jj) if direction == 0 else (ii <= jj)
    neg_a = -jnp.exp(alog_ref[...])

    lane_lo = lax.broadcasted_iota(jnp.int32, (c, LANES), 1) < SSD_HEAD_DIM
    hm_b = hm_ref[...].astype(BF16)
    last = c - 1 if direction == 0 else 0
    d0, d1 = SMALL_DT_LANE, SMALL_DT_LANE + SSD_HEADS

    for b in range(nb):
        raw = sm_ref[b] + dtb_ref[...]
        dt = jnp.maximum(raw, 0.0) + jnp.log1p(jnp.exp(-jnp.abs(raw)))
        g = _dot_sel_rhs(tri_ref[...], dt * neg_a)
        gt8 = g.T[d0:d1, :]
        dt8 = dt.T[d0:d1, :]
        w8 = dt8 * jnp.exp(gt8[:, last:last + 1] - gt8)
        gl_x = _dot_sel_lhs(jnp.broadcast_to(jnp.exp(g[last:last + 1, :]), (8, LANES)), x_ref[...])[0:1, :]

        xs = xc_ref[b, :, 0:SSD_W]
        xs_b = xs.astype(BF16)
        outs = []
        for grp in range(SSD_GROUPS):
            b0 = SSD_W + grp * SSD_STATE
            c0 = SSD_W + SSD_GROUPS * SSD_STATE + grp * SSD_STATE
            bm = xc_ref[b, :, b0:b0 + SSD_STATE]
            bm_t = bm.T
            cm = xc_ref[b, :, c0:c0 + SSD_STATE].astype(BF16)
            cb = _dot_nt(cm, bm.astype(BF16))
            xs_g = xs_b[:, grp * gw:(grp + 1) * gw]
            a_parts, k_parts, x_parts, col_parts = [], [], [], []
            for hh in range(hpg):
                hd = grp * hpg + hh
                col = jnp.broadcast_to(g[:, d0 + hd:d0 + hd + 1], (c, c))
                dec = jnp.where(tri_mask, jnp.exp(jnp.minimum(col - gt8[hd:hd + 1, :], 0.0)), 0.0)
                a_parts.append((cb * (dec * dt8[hd:hd + 1, :])).astype(BF16))
                k_parts.append((bm_t * w8[hd:hd + 1, :]).astype(BF16))
                x_parts.append(xs_g * hm_b[hh:hh + 1, :])
                col_parts.append(col)
            x_bd = jnp.concatenate(x_parts, axis=0)
            intra = _dot(jnp.concatenate(a_parts, axis=1), x_bd)
            e_in = jnp.exp(jnp.concatenate(
                [jnp.where(lane_lo, col_parts[2 * t], col_parts[2 * t + 1]) for t in range(hpg // 2)], axis=1))
            st = st_ref[b, grp]
            outs.append(intra + e_in * _dot(cm, st.astype(BF16)))
            st_ref[b, grp] = (st * gl_x[:, grp * gw:(grp + 1) * gw]
                              + _dot(jnp.concatenate(k_parts, axis=1), x_bd))
        o = jnp.concatenate(outs, axis=1)
        if direction == 0:
            o_ref[b] = o
        else:
            y = (of_ref[b] + o + dsk_ref[...] * xs) * _silu(z_ref[b])
            o_ref[b] = _rms(y, nw_ref[...]).astype(BF16)


def _ssd_call(direction, xconv, small_d, p_z, o_f, dtb_d, alog_d, dsk, nw, consts, *, nctx):
    nb, rb, _ = xconv.shape
    c = SSD_CHUNK
    assert c == LANES
    nch, ncc = rb // c, nctx // c
    kern = functools.partial(_ssd_kernel, direction=direction, nb=nb, c=c)
    chunk = lambda n: (0, _chunk_of(direction, n, ncc, nch), 0)
    const2 = lambda n: (0, 0)
    blk = lambda w: pl.BlockSpec((nb, c, w), chunk)
    full = lambda a: pl.BlockSpec(a.shape, const2)
    ins = [xconv, small_d]
    specs = [blk(SSD_CONV_CH), blk(LANES)]
    if direction == 1:
        ins += [p_z, o_f]
        specs += [blk(SSD_W), blk(SSD_W)]
    tail = [dtb_d, alog_d, consts["tri_ssd"][direction], consts["ssd_x"], consts["hmask"]]
    if direction == 1:
        tail += [dsk, nw]
    ins += tail
    specs += [full(a) for a in tail]
    return pl.pallas_call(
        kern,
        grid=(nch,),
        in_specs=specs,
        out_specs=blk(SSD_W),
        out_shape=jax.ShapeDtypeStruct((nb, rb, SSD_W), F32 if direction == 0 else BF16),
        scratch_shapes=[pltpu.VMEM((nb, SSD_GROUPS, SSD_STATE, 256), F32)],
        compiler_params=_scan_params(),
        name="ssd_fwd" if direction == 0 else "ssd_bwd",
    )(*ins)


def _ret_log_gamma_row(width):
    lane = lax.broadcasted_iota(jnp.int32, (1, width), 1)
    row = jnp.zeros((1, width), F32)
    for h in range(RET_HEADS):
        lg = math.log1p(-(2.0 ** (-5.0 - h)))
        row = jnp.where(lane // RET_HEAD_DIM == h, lg, row)
    return row


def _rope(t, cos, sin_signed):
    lane = lax.broadcasted_iota(jnp.int32, t.shape, 1)
    half = RET_HEAD_DIM // 2
    swapped = jnp.where(lane % RET_HEAD_DIM < half,
                        pltpu.roll(t, LANES - half, 1), pltpu.roll(t, half, 1))
    return t * cos + swapped * sin_signed


def _ret_kernel(*refs, direction, nb, c):
    if direction == 0:
        (p_ref, cos_ref, sin_ref, avg_ref, hm_ref, o_ref, st_ref) = refs
    else:
        (p_ref, cos_ref, sin_ref, of_ref, avg_ref, hm_ref, nw_ref, o_ref, st_ref) = refs
    w = RET_W

    @pl.when(pl.program_id(0) == 0)
    def _():
        st_ref[...] = jnp.zeros_like(st_ref)

    cos = cos_ref[...]
    sin = sin_ref[...]
    lg_row = _ret_log_gamma_row(w)
    pos = lax.broadcasted_iota(jnp.int32, (c, 1), 0)
    cnt_in = (pos + 1 if direction == 0 else c - pos).astype(F32)
    cnt_st = (c - 1 - pos if direction == 0 else pos).astype(F32)
    e_in = jnp.exp(cnt_in * lg_row)
    e_st = jnp.exp(cnt_st * lg_row)
    e_all = jnp.exp(float(c) * lg_row)
    ii = lax.broadcasted_iota(jnp.int32, (c, c), 0)
    jj = lax.broadcasted_iota(jnp.int32, (c, c), 1)
    dist = (ii - jj) if direction == 0 else (jj - ii)
    dec_parts = []
    for h in range(RET_HEADS):
        lg = math.log1p(-(2.0 ** (-5.0 - h)))
        dec_parts.append(jnp.where(dist >= 0, jnp.exp(dist.astype(F32) * lg), 0.0))
    dec = jnp.concatenate(dec_parts, axis=1)
    bd_mask = avg_ref[...].astype(F32)

    for b in range(nb):
        q_halves, k_halves = [], []
        for hf in range(w // LANES):
            qh = p_ref[b, :, hf * LANES:(hf + 1) * LANES] * (RET_HEAD_DIM ** -0.5)
            kh = p_ref[b, :, w + hf * LANES:w + (hf + 1) * LANES]
            q_halves.append(_rope(qh, cos, sin))
            k_halves.append(_rope(kh, cos, sin))
        q = jnp.concatenate(q_halves, axis=1)
        k = jnp.concatenate(k_halves, axis=1)
        v = p_ref[b, :, 2 * w:3 * w]
        qb = q.astype(BF16)
        k_parts, v_parts = [], []
        for h in range(RET_HEADS):
            hm = hm_ref[h:h + 1, :]
            k_parts.append((k * hm).astype(BF16))
            v_parts.append((v * hm).astype(BF16))
        k_bd = jnp.concatenate(k_parts, axis=0)
        v_bd = jnp.concatenate(v_parts, axis=0)
        scores = _dot_nt(qb, k_bd) * dec
        intra = _dot(scores.astype(BF16), v_bd)
        st = st_ref[b]
        o = intra + e_in * _dot(qb, st.astype(BF16))
        st_ref[b] = st * e_all + _dot_tn((k * e_st).astype(BF16), v.astype(BF16)) * bd_mask
        if direction == 0:
            o_ref[b] = o
        else:
            tot = of_ref[b] + o
            mu = _dot_sel_lhs(tot, avg_ref[...], 2) * (1.0 / RET_HEAD_DIM)
            xc = tot - mu
            var = _dot_sel_lhs(xc * xc, avg_ref[...], 2) * (1.0 / RET_HEAD_DIM)
            y = xc * lax.rsqrt(var + RMS_EPS) * nw_ref[...]
            o_ref[b] = (y * _silu(p_ref[b, :, 3 * w:4 * w])).astype(BF16)


def _ret_call(direction, p_ret, cos_t, sin_t, o_f, nw, consts, *, nctx):
    nb, rb, _ = p_ret.shape
    c = RET_CHUNK
    nch, ncc = rb // c, nctx // c
    kern = functools.partial(_ret_kernel, direction=direction, nb=nb, c=c)
    chunk = lambda n: (0, _chunk_of(direction, n, ncc, nch), 0)
    tchunk = lambda n: (_chunk_of(direction, n, ncc, nch), 0)
    const2 = lambda n: (0, 0)
    blk = lambda w: pl.BlockSpec((nb, c, w), chunk)
    full = lambda a: pl.BlockSpec(a.shape, const2)
    ins = [p_ret, cos_t, sin_t]
    specs = [blk(4 * RET_W), pl.BlockSpec((c, LANES), tchunk), pl.BlockSpec((c, LANES), tchunk)]
    if direction == 1:
        ins.append(o_f)
        specs.append(blk(RET_W))
    tail = [consts["avg256"], consts["hmask"]]
    if direction == 1:
        tail.append(nw)
    ins += tail
    specs += [full(a) for a in tail]
    return pl.pallas_call(
        kern,
        grid=(nch,),
        in_specs=specs,
        out_specs=blk(RET_W),
        out_shape=jax.ShapeDtypeStruct((nb, rb, RET_W), F32 if direction == 0 else BF16),
        scratch_shapes=[pltpu.VMEM((nb, RET_W, RET_W), F32)],
        compiler_params=_scan_params(),
        name="ret_fwd" if direction == 0 else "ret_bwd",
    )(*ins)


def _outproj_kernel(h_ref, mod_ref, gla_ref, ssd_ref, ret_ref, wo_ref, nw_ref, o_ref, *, tm, spb, ncs, nb):
    i = pl.program_id(0)
    d = D_MODEL
    mixed = jnp.concatenate([gla_ref[...], ssd_ref[...], ret_ref[...]], axis=1)
    m = _dot(mixed, wo_ref[...])
    for k in range(tm // SLAB):
        mrow = _mod_row(i * (tm // SLAB) + k, spb, ncs, nb)
        gate = mod_ref[mrow][:, 2 * d:3 * d]
        sl = slice(k * SLAB, (k + 1) * SLAB)
        o_ref[sl, :] = h_ref[sl, :] + gate * _rms(m[sl, :], nw_ref[...])


def _outproj_call(h, mod, gla, ssd, ret, wo, nw, *, spb, ncs, nb):
    r, d = h.shape
    tm = TM_DENSE
    kern = functools.partial(_outproj_kernel, tm=tm, spb=spb, ncs=ncs, nb=nb)
    row = lambda i: (i, 0)
    const = lambda i: (0, 0)
    return pl.pallas_call(
        kern,
        grid=(r // tm,),
        in_specs=[
            pl.BlockSpec((tm, d), row),
            pl.BlockSpec(mod.shape, lambda i: (0, 0, 0)),
            pl.BlockSpec((tm, GLA_W), row),
            pl.BlockSpec((tm, SSD_W), row),
            pl.BlockSpec((tm, RET_W), row),
            pl.BlockSpec(wo.shape, const),
            pl.BlockSpec((1, d), const),
        ],
        out_specs=pl.BlockSpec((tm, d), row),
        out_shape=jax.ShapeDtypeStruct((r, d), F32),
        compiler_params=pltpu.CompilerParams(
            dimension_semantics=("arbitrary",), vmem_limit_bytes=VMEM_LIMIT),
        name="out_proj",
    )(h, mod, gla, ssd, ret, wo, nw)


def _ffn_kernel(h_ref, mod_ref, npre_ref, w13_ref, w2_ref, npost_ref, o_ref, hb_ref, acc_ref,
                *, tm, spb, ncs, nb):
    i = pl.program_id(0)
    d = D_MODEL
    for k in range(tm // SLAB):
        mrow = _mod_row(i * (tm // SLAB) + k, spb, ncs, nb)
        shift = mod_ref[mrow][:, 3 * d:4 * d]
        scale = mod_ref[mrow][:, 4 * d:5 * d]
        sl = slice(k * SLAB, (k + 1) * SLAB)
        y = _rms(h_ref[sl, :], npre_ref[...]) * (1.0 + scale) + shift
        hb_ref[sl, :] = y.astype(BF16)
    hb = hb_ref[...]
    for cidx in range(FFN_HIDDEN // FFN_CHUNK):
        lo = cidx * FFN_CHUNK
        gate = _dot(hb, w13_ref[:, lo:lo + FFN_CHUNK])
        up = _dot(hb, w13_ref[:, FFN_HIDDEN + lo:FFN_HIDDEN + lo + FFN_CHUNK])
        act = (_silu(gate) * up).astype(BF16)
        part = _dot(act, w2_ref[lo:lo + FFN_CHUNK, :])
        if cidx == 0:
            acc_ref[...] = part
        else:
            acc_ref[...] += part
    for k in range(tm // SLAB):
        mrow = _mod_row(i * (tm // SLAB) + k, spb, ncs, nb)
        gate = mod_ref[mrow][:, 5 * d:6 * d]
        sl = slice(k * SLAB, (k + 1) * SLAB)
        o_ref[sl, :] = h_ref[sl, :] + gate * _rms(acc_ref[sl, :], npost_ref[...])


def _ffn_call(h, mod, npre, w13, w2, npost, *, spb, ncs, nb):
    r, d = h.shape
    tm = TM_DENSE
    kern = functools.partial(_ffn_kernel, tm=tm, spb=spb, ncs=ncs, nb=nb)
    row = lambda i: (i, 0)
    const = lambda i: (0, 0)
    return pl.pallas_call(
        kern,
        grid=(r // tm,),
        in_specs=[
            pl.BlockSpec((tm, d), row),
            pl.BlockSpec(mod.shape, lambda i: (0, 0, 0)),
            pl.BlockSpec((1, d), const),
            pl.BlockSpec(w13.shape, const),
            pl.BlockSpec(w2.shape, const),
            pl.BlockSpec((1, d), const),
        ],
        out_specs=pl.BlockSpec((tm, d), row),
        out_shape=jax.ShapeDtypeStruct((r, d), F32),
        scratch_shapes=[pltpu.VMEM((tm, d), BF16), pltpu.VMEM((tm, d), F32)],
        compiler_params=pltpu.CompilerParams(
            dimension_semantics=("arbitrary",), vmem_limit_bytes=VMEM_LIMIT),
        name="ffn",
    )(h, mod, npre, w13, w2, npost)


def _tri_pair(c):
    i = jnp.arange(c)[:, None]
    j = jnp.arange(c)[None, :]
    return jnp.stack([(j <= i), (j >= i)]).astype(BF16)


def _gla_cum_pair(cg, sb):
    i = jnp.arange(cg)[:, None]
    j = jnp.arange(cg)[None, :]
    same = (i // sb) == (j // sb)
    fwd = jnp.concatenate([same & (j <= i), same], axis=0)
    bwd = jnp.concatenate([same & (j >= i), same], axis=0)
    return jnp.stack([fwd, bwd]).astype(BF16)


def _constants():
    lane256 = jnp.arange(256)
    consts = {
        "gla_cum": _gla_cum_pair(GLA_BLOCK, GLA_SUB),
        "tri_ssd": _tri_pair(SSD_CHUNK),
        "avg256": (lane256[:, None] // 64 == lane256[None, :] // 64).astype(BF16),
        "gla_e2": (jnp.arange(GLA_KDIM)[:, None] // GLA_HEAD_K == lane256[None, :] // GLA_HEAD_V).astype(BF16),
        "gla_bdm": (lane256[:, None] // GLA_HEAD_V == jnp.arange(GLA_KDIM)[None, :] // GLA_HEAD_K).astype(F32),
        "ssd_x": (jnp.arange(LANES)[:, None] - SMALL_DT_LANE
                  == jnp.arange(SSD_W)[None, :] // SSD_HEAD_DIM).astype(BF16),
        "hmask": (jnp.arange(8)[:, None] == lane256[None, :] // 64).astype(F32),
    }
    return consts


def _rope_tables(t_lat, nctx):
    rows = t_lat // GRID_W
    row = jnp.repeat(jnp.arange(rows), GRID_W).astype(F32)
    col = jnp.tile(jnp.arange(GRID_W), rows).astype(F32)
    n_freq = RET_HEAD_DIM // 4
    inv_freq = ROPE_BASE ** (-jnp.arange(n_freq, dtype=F32) / n_freq)
    ang = jnp.concatenate([row[:, None] * inv_freq, col[:, None] * inv_freq], axis=-1)
    cos = jnp.cos(ang)
    sin = jnp.sin(ang)
    cos_t = jnp.concatenate([cos, cos, cos, cos], axis=-1)
    sin_t = jnp.concatenate([-sin, sin, -sin, sin], axis=-1)
    cos_t = jnp.concatenate([jnp.ones((nctx, LANES), F32), cos_t], axis=0)
    sin_t = jnp.concatenate([jnp.zeros((nctx, LANES), F32), sin_t], axis=0)
    return cos_t, sin_t


def kernel(x, c, ctx, c_ctx, ada_w, ada_b, norm_mix_pre, norm_mix_post, norm_ffn_pre, norm_ffn_post,
           w_in, w_out, gla_gate_up, gla_gate_b, gla_norm, ssd_conv_w, ssd_conv_b, ssd_dt_bias,
           ssd_a_log, ssd_d, ssd_norm, ret_norm, ffn_w13, ffn_w2):
    nb, t_lat, d = x.shape
    nctx = ctx.shape[1]
    depth = ada_w.shape[0]
    rb = nctx + t_lat
    r = nb * rb
    spb = rb // SLAB
    ncs = nctx // SLAB
    assert d == D_MODEL and nb + 1 <= 8
    assert nctx % SLAB == 0 and t_lat % SLAB == 0 and r % TM_DENSE == 0

    consts = _constants()
    cos_t, sin_t = _rope_tables(t_lat, nctx)

    cvec = jnp.zeros((8, d), F32).at[:nb].set(c).at[nb].set(c_ctx)
    mods = _ada_call(cvec, ada_w, ada_b)

    s1, s2 = GLA_COLS, GLA_COLS + SSD_COLS
    lr0 = 2 * GLA_KDIM + 2 * GLA_W
    dt0 = s1 + SSD_W + SSD_CONV_CH
    wm = jnp.concatenate([w_in[:, :, 0:lr0], w_in[:, :, s1:dt0], w_in[:, :, s2:]], axis=-1).astype(BF16)
    zpad = jnp.zeros((depth, d, LANES - GLA_RANK - SSD_HEADS), F32)
    ws = jnp.concatenate(
        [w_in[:, :, lr0:lr0 + GLA_RANK], w_in[:, :, dt0:dt0 + SSD_HEADS], zpad,
         w_in[:, :, lr0 + GLA_RANK:lr0 + 2 * GLA_RANK], w_in[:, :, dt0 + SSD_HEADS:dt0 + 2 * SSD_HEADS], zpad],
        axis=-1).astype(BF16)
    wo = w_out.astype(BF16)
    w13 = ffn_w13.astype(BF16)
    w2 = ffn_w2.astype(BF16)

    wg = jnp.zeros((depth, 2, LANES, GLA_KDIM), F32).at[:, :, :GLA_RANK, :].set(gla_gate_up)
    gb = gla_gate_b.reshape(depth, 2, 1, GLA_KDIM)
    lane_dt = slice(SMALL_DT_LANE, SMALL_DT_LANE + SSD_HEADS)
    dtb = jnp.zeros((depth, 2, 1, LANES), F32).at[:, :, 0, lane_dt].set(ssd_dt_bias)
    alog = jnp.zeros((depth, 2, 1, LANES), F32).at[:, :, 0, lane_dt].set(ssd_a_log)
    dsk = jnp.repeat(ssd_d, SSD_HEAD_DIM, axis=-1).reshape(depth, 1, SSD_W)
    conv_w = jnp.zeros((depth, 8, SSD_CONV_CH), F32).at[:, :SSD_CONV_W].set(ssd_conv_w)

    h = jnp.concatenate([ctx, x], axis=1).reshape(r, d)
    meta = dict(spb=spb, ncs=ncs, nb=nb)
    row1 = lambda a: a.reshape(1, -1)
    for l in range(depth):
        mod = mods[l].reshape(8, 1, 6 * d)
        p_gla, p_z, xconv, p_ret, small = _inproj_call(
            h, mod, row1(norm_mix_pre[l]), wm[l], ws[l], conv_w[l], row1(ssd_conv_b[l]), **meta)
        small = small.reshape(2, nb, rb, LANES)
        xconv = xconv.reshape(nb, rb, SSD_CONV_CH)
        p_gla = p_gla.reshape(nb, rb, 768)
        p_z = p_z.reshape(nb, rb, SSD_W)
        p_ret = p_ret.reshape(nb, rb, 4 * RET_W)
        gla_f = _gla_call(0, p_gla, small[0], None, wg[l, 0], gb[l, 0], None, consts, nctx=nctx)
        gla = _gla_call(1, p_gla, small[1], gla_f, wg[l, 1], gb[l, 1], row1(gla_norm[l]), consts, nctx=nctx)
        ssd_f = _ssd_call(0, xconv, small[0], None, None, dtb[l, 0], alog[l, 0], None, None, consts, nctx=nctx)
        ssd = _ssd_call(1, xconv, small[1], p_z, ssd_f, dtb[l, 1], alog[l, 1], dsk[l], row1(ssd_norm[l]),
                        consts, nctx=nctx)
        ret_f = _ret_call(0, p_ret, cos_t, sin_t, None, None, consts, nctx=nctx)
        ret = _ret_call(1, p_ret, cos_t, sin_t, ret_f, row1(ret_norm[l]), consts, nctx=nctx)
        h = _outproj_call(h, mod, gla.reshape(r, GLA_W), ssd.reshape(r, SSD_W), ret.reshape(r, RET_W),
                          wo[l], row1(norm_mix_post[l]), **meta)
        h = _ffn_call(h, mod, row1(norm_ffn_pre[l]), w13[l], w2[l], row1(norm_ffn_post[l]), **meta)
    return h.reshape(nb, rb, d)[:, nctx:, :]
```

```python
import functools
import math

import jax
import jax.numpy as jnp
from jax import lax
from jax.experimental import pallas as pl
from jax.experimental.pallas import tpu as pltpu

F32 = jnp.float32
BF16 = jnp.bfloat16

D_MODEL = 1024
GRID_W = 64
RMS_EPS = 1e-6
GLA_W = 256
GLA_HEADS = 4
GLA_HEAD_V = 64
GLA_HEAD_K = 32
GLA_KDIM = 128
GLA_RANK = 16
GLA_GATE_TAU = 16.0
SSD_W = 512
SSD_HEADS = 8
SSD_HEAD_DIM = 64
SSD_GROUPS = 2
SSD_STATE = 128
SSD_CONV_W = 5
SSD_CONV_CH = 1024
RET_W = 256
RET_HEADS = 4
RET_HEAD_DIM = 64
ROPE_BASE = 10000.0
GLA_COLS = 800
SSD_COLS = 1552
FFN_HIDDEN = 2816

LANES = 128
SLAB = 256
TM_DENSE = 512
FFN_CHUNK = 256
GLA_SUB = 16
GLA_BLOCK = 256
SSD_CHUNK = 128
RET_CHUNK = 128
SMALL_DT_LANE = 16
VMEM_LIMIT = 56 * 1024 * 1024


def _silu(x):
    return x * jax.nn.sigmoid(x)


def _dot(a, b):
    return jnp.dot(a, b, preferred_element_type=F32)


def _dot_nt(a, b):
    return lax.dot_general(a, b, (((1,), (1,)), ((), ())), preferred_element_type=F32)


def _dot_tn(a, b):
    return lax.dot_general(a, b, (((0,), (0,)), ((), ())), preferred_element_type=F32)


def _split_bf16(x, terms):
    parts = []
    r = x
    for _ in range(terms):
        hi = r.astype(BF16)
        parts.append(hi)
        r = r - hi.astype(F32)
    return parts


def _dot_sel_rhs(m01, x, terms=3):
    acc = None
    for p in _split_bf16(x, terms):
        t = _dot(m01, p)
        acc = t if acc is None else acc + t
    return acc


def _dot_sel_lhs(x, m01, terms=3):
    acc = None
    for p in _split_bf16(x, terms):
        t = _dot(p, m01)
        acc = t if acc is None else acc + t
    return acc


def _dot3(a, b):
    a_hi = a.astype(BF16)
    a_lo = (a - a_hi.astype(F32)).astype(BF16)
    b_hi = b.astype(BF16)
    b_lo = (b - b_hi.astype(F32)).astype(BF16)
    return _dot(a_hi, b_hi) + _dot(a_lo, b_hi) + _dot(a_hi, b_lo)


def _mod_row(slab, spb, ncs, nb):
    return jnp.where(slab % spb < ncs, nb, slab // spb)


def _rms(x, w):
    return x * lax.rsqrt(jnp.mean(x * x, axis=-1, keepdims=True) + RMS_EPS) * w


def _ada_kernel(cv_ref, w_ref, b_ref, o_ref):
    s = _silu(cv_ref[...]).astype(BF16)
    o_ref[...] = _dot(s, w_ref[...].astype(BF16)) + b_ref[...]


def _ada_call(cvec, ada_w, ada_b):
    depth, d, n6 = ada_w.shape
    tn = 2048
    return pl.pallas_call(
        _ada_kernel,
        grid=(depth, n6 // tn),
        in_specs=[
            pl.BlockSpec((8, d), lambda l, j: (0, 0)),
            pl.BlockSpec((None, d, tn), lambda l, j: (l, 0, j)),
            pl.BlockSpec((None, 1, tn), lambda l, j: (l, 0, j)),
        ],
        out_specs=pl.BlockSpec((None, 8, tn), lambda l, j: (l, 0, j)),
        out_shape=jax.ShapeDtypeStruct((depth, 8, n6), F32),
        compiler_params=pltpu.CompilerParams(
            dimension_semantics=("arbitrary", "arbitrary"), vmem_limit_bytes=VMEM_LIMIT),
        name="ada_mod",
    )(cvec, ada_w, ada_b.reshape(depth, 1, n6))


def _inproj_kernel(h_ref, hp_ref, hn_ref, mod_ref, nw_ref, wm_ref, ws_ref, cw_ref, cb_ref,
                   gla_ref, z_ref, xc_ref, ret_ref, small_ref, yb_ref, scr_ref, *, tm, spb, ncs, nb):
    i = pl.program_id(0)
    d = D_MODEL
    nsl = tm // SLAB
    x0, x1 = 1280, 2304

    def normed(x, mrow):
        return _rms(x, nw_ref[...]) * (1.0 + mod_ref[mrow][:, d:2 * d]) + mod_ref[mrow][:, 0:d]

    mrows = [_mod_row(i * nsl + k, spb, ncs, nb) for k in range(nsl)]
    for k in range(nsl):
        yb_ref[k * SLAB:(k + 1) * SLAB, :] = normed(h_ref[k * SLAB:(k + 1) * SLAB, :], mrows[k]).astype(BF16)
    yb = yb_ref[...]
    pm = _dot(yb, wm_ref[...])
    gla_ref[...] = pm[:, 0:768]
    z_ref[...] = pm[:, 768:x0]
    ret_ref[...] = pm[:, x1:3328]
    xbc = pm[:, x0:x1]
    ps = _dot(yb, ws_ref[...])
    small_ref[0] = ps[:, 0:LANES]
    small_ref[1] = ps[:, LANES:2 * LANES]

    yh = jnp.concatenate([normed(hp_ref[...], mrows[0]), normed(hn_ref[...], mrows[nsl - 1])], axis=0)
    halo = _dot(yh.astype(BF16), wm_ref[:, x0:x1])
    pad = (SSD_CONV_W - 1) // 2
    for k in range(nsl):
        pos = (i * nsl + k) % spb
        first = jnp.logical_or(pos == 0, pos == ncs)
        last = jnp.logical_or(pos == ncs - 1, pos == spb - 1)
        before = halo[0:8] if k == 0 else xbc[k * SLAB - 8:k * SLAB, :]
        after = halo[8:16] if k == nsl - 1 else xbc[(k + 1) * SLAB:(k + 1) * SLAB + 8, :]
        scr_ref[0:8, :] = jnp.where(first, 0.0, before)
        scr_ref[8:8 + SLAB, :] = xbc[k * SLAB:(k + 1) * SLAB, :]
        scr_ref[8 + SLAB:16 + SLAB, :] = jnp.where(last, 0.0, after)
        acc = cb_ref[...] + cw_ref[0:1, :] * scr_ref[pl.ds(8 - pad, SLAB), :]
        for t in range(1, SSD_CONV_W):
            acc = acc + cw_ref[t:t + 1, :] * scr_ref[pl.ds(8 - pad + t, SLAB), :]
        xc_ref[k * SLAB:(k + 1) * SLAB, :] = _silu(acc)


def _inproj_call(h, mod, nw, wm, ws, cw, cb, *, spb, ncs, nb):
    r, d = h.shape
    tm = TM_DENSE
    kern = functools.partial(_inproj_kernel, tm=tm, spb=spb, ncs=ncs, nb=nb)
    row = lambda i: (i, 0)
    const = lambda i: (0, 0)
    per = tm // 8
    return pl.pallas_call(
        kern,
        grid=(r // tm,),
        in_specs=[
            pl.BlockSpec((tm, d), row),
            pl.BlockSpec((8, d), lambda i: (jnp.maximum(i * per - 1, 0), 0)),
            pl.BlockSpec((8, d), lambda i: (jnp.minimum((i + 1) * per, r // 8 - 1), 0)),
            pl.BlockSpec(mod.shape, lambda i: (0, 0, 0)),
            pl.BlockSpec((1, d), const),
            pl.BlockSpec(wm.shape, const),
            pl.BlockSpec(ws.shape, const),
            pl.BlockSpec(cw.shape, const),
            pl.BlockSpec(cb.shape, const),
        ],
        out_specs=[
            pl.BlockSpec((tm, 768), row),
            pl.BlockSpec((tm, 512), row),
            pl.BlockSpec((tm, 1024), row),
            pl.BlockSpec((tm, 1024), row),
            pl.BlockSpec((2, tm, LANES), lambda i: (0, i, 0)),
        ],
        out_shape=[
            jax.ShapeDtypeStruct((r, 768), F32),
            jax.ShapeDtypeStruct((r, 512), F32),
            jax.ShapeDtypeStruct((r, 1024), F32),
            jax.ShapeDtypeStruct((r, 1024), F32),
            jax.ShapeDtypeStruct((2, r, LANES), F32),
        ],
        scratch_shapes=[pltpu.VMEM((tm, d), BF16), pltpu.VMEM((SLAB + 16, SSD_CONV_CH), F32)],
        compiler_params=pltpu.CompilerParams(
            dimension_semantics=("arbitrary",), vmem_limit_bytes=VMEM_LIMIT),
        name="in_proj",
    )(h, h, h, mod, nw, wm, ws, cw, cb)


def _chunk_of(direction, n, ncc, nch):
    if direction == 0:
        return n
    return jnp.where(n < ncc, ncc - 1 - n, nch + ncc - 1 - n)


def _scan_params():
    return pltpu.CompilerParams(dimension_semantics=("arbitrary",), vmem_limit_bytes=VMEM_LIMIT)


def _gla_kernel(*refs, direction, nb, cg):
    if direction == 0:
        (p_ref, sm_ref, wg_ref, gb_ref, cum_ref, e2_ref, bdm_ref, o_ref, st_ref) = refs
    else:
        (p_ref, sm_ref, of_ref, wg_ref, gb_ref, cum_ref, e2_ref, bdm_ref, avg_ref, nw_ref,
         o_ref, st_ref) = refs
    sb = GLA_SUB
    nsb = cg // sb

    @pl.when(pl.program_id(0) == 0)
    def _():
        st_ref[...] = jnp.zeros_like(st_ref)

    rows16 = lax.broadcasted_iota(jnp.int32, (sb, GLA_KDIM), 0)
    masks = [(rows16 >= j) if direction == 0 else (rows16 <= j) for j in range(sb)]

    pre = []
    for b in range(nb):
        z = _dot3(sm_ref[b], wg_ref[...]) + gb_ref[...]
        logg = (jnp.minimum(z, 0.0) - jnp.log1p(jnp.exp(-jnp.abs(z)))) * (1.0 / GLA_GATE_TAU)
        cums = _dot_sel_rhs(cum_ref[...], logg)
        gs_all = cums[0:cg]
        gl = cums[cg:2 * cg]
        qs_all = p_ref[b, :, 0:GLA_KDIM] * (GLA_HEAD_K ** -0.5)
        k_all = p_ref[b, :, GLA_KDIM:2 * GLA_KDIM]
        v_all = p_ref[b, :, 2 * GLA_KDIM:2 * GLA_KDIM + GLA_W]
        pre.append(dict(
            gs=gs_all, egl=jnp.exp(gl), qs=qs_all, k=k_all, v=v_all,
            qt=(qs_all * jnp.exp(gs_all)).astype(BF16),
            kk=(k_all * jnp.exp(gl - gs_all)).astype(BF16),
            vb=v_all.astype(BF16)))

    sts = [st_ref[b] for b in range(nb)]
    o_parts = [[None] * nsb for _ in range(nb)]
    for it in range(nsb):
        a = it if direction == 0 else nsb - 1 - it
        sl = slice(a * sb, (a + 1) * sb)
        for b in range(nb):
            pb = pre[b]
            gs, qs, kb, vb = pb["gs"][sl], pb["qs"][sl], pb["k"][sl], pb["v"][sl]
            inter = _dot_nt(pb["qt"][sl], sts[b].astype(BF16))
            parts = []
            for j in range(sb):
                dec = jnp.where(masks[j], jnp.exp(gs - gs[j:j + 1, :]), 0.0)
                parts.append((dec * qs * kb[j:j + 1, :]).astype(BF16))
            sx = _dot(jnp.concatenate(parts, axis=0), e2_ref[...])
            intra = sx[0:sb] * vb[0:1, :]
            for j in range(1, sb):
                intra = intra + sx[j * sb:(j + 1) * sb] * vb[j:j + 1, :]
            o_parts[b][a] = intra + inter
            sts[b] = (sts[b] * pb["egl"][a * sb:a * sb + 1, :]
                      + _dot_tn(pb["vb"][sl], pb["kk"][sl]) * bdm_ref[...])
    for b in range(nb):
        st_ref[b] = sts[b]
    outs = [jnp.concatenate(o_parts[b], axis=0) for b in range(nb)]
    if direction == 0:
        for b in range(nb):
            o_ref[b] = outs[b]
    else:
        tots = [of_ref[b] + outs[b] for b in range(nb)]
        mss = [_dot_sel_lhs(tots[b] * tots[b], avg_ref[...], 2) * (1.0 / GLA_HEAD_V) for b in range(nb)]
        for b in range(nb):
            y = tots[b] * lax.rsqrt(mss[b] + RMS_EPS) * nw_ref[...]
            r = p_ref[b, :, 2 * GLA_KDIM + GLA_W:2 * GLA_KDIM + 2 * GLA_W]
            o_ref[b] = (y * _silu(r)).astype(BF16)


def _gla_call(direction, p_gla, small_d, o_f, wg_d, gb_d, nw, consts, *, nctx):
    nb, rb, _ = p_gla.shape
    cg = GLA_BLOCK
    nch, ncc = rb // cg, nctx // cg
    kern = functools.partial(_gla_kernel, direction=direction, nb=nb, cg=cg)
    chunk = lambda n: (0, _chunk_of(direction, n, ncc, nch), 0)
    const2 = lambda n: (0, 0)
    blk = lambda w: pl.BlockSpec((nb, cg, w), chunk)
    full = lambda a: pl.BlockSpec(a.shape, const2)
    cum = consts["gla_cum"][direction]
    ins = [p_gla, small_d]
    specs = [blk(768), blk(LANES)]
    if direction == 1:
        ins.append(o_f)
        specs.append(blk(GLA_W))
    tail = [wg_d, gb_d, cum, consts["gla_e2"], consts["gla_bdm"]]
    if direction == 1:
        tail += [consts["avg256"], nw]
    ins += tail
    specs += [full(a) for a in tail]
    scratch = [pltpu.VMEM((nb, GLA_W, GLA_KDIM), F32)]
    return pl.pallas_call(
        kern,
        grid=(nch,),
        in_specs=specs,
        out_specs=blk(GLA_W),
        out_shape=jax.ShapeDtypeStruct((nb, rb, GLA_W), F32 if direction == 0 else BF16),
        scratch_shapes=scratch,
        compiler_params=_scan_params(),
        name="gla_fwd" if direction == 0 else "gla_bwd",
    )(*ins)


def _ssd_kernel(*refs, direction, nb, c):
    if direction == 0:
        (xc_ref, sm_ref, dtb_ref, alog_ref, tri_ref, x_ref, hm_ref, o_ref, st_ref) = refs
    else:
        (xc_ref, sm_ref, z_ref, of_ref, dtb_ref, alog_ref, tri_ref, x_ref, hm_ref, dsk_ref, nw_ref,
         o_ref, st_ref) = refs
    hpg = SSD_HEADS // SSD_GROUPS
    gw = hpg * SSD_HEAD_DIM

    @pl.when(pl.program_id(0) == 0)
    def _():
        st_ref[...] = jnp.zeros_like(st_ref)

    ii = lax.broadcasted_iota(jnp.int32, (c, c), 0)
    jj = lax.broadcasted_iota(jnp.int32, (c, c), 1)
    tri_mask = (ii >= jj) if direction == 0 else (ii <= jj)
    neg_a = -jnp.exp(alog_ref[...])

    lane_lo = lax.broadcasted_iota(jnp.int32, (c, LANES), 1) < SSD_HEAD_DIM
    hm_b = hm_ref[...].astype(BF16)
    last = c - 1 if direction == 0 else 0
    d0, d1 = SMALL_DT_LANE, SMALL_DT_LANE + SSD_HEADS

    pairs = [(b, grp) for b in range(nb) for grp in range(SSD_GROUPS)]
    dts = []
    for b in range(nb):
        raw = sm_ref[b] + dtb_ref[...]
        dts.append(jnp.maximum(raw, 0.0) + jnp.log1p(jnp.exp(-jnp.abs(raw))))
    gs = [_dot_sel_rhs(tri_ref[...], dts[b] * neg_a) for b in range(nb)]
    gt8s = [gs[b].T[d0:d1, :] for b in range(nb)]
    dt8s = [dts[b].T[d0:d1, :] for b in range(nb)]
    w8s = [dt8s[b] * jnp.exp(gt8s[b][:, last:last + 1] - gt8s[b]) for b in range(nb)]
    gl_xs = [_dot_sel_lhs(jnp.broadcast_to(jnp.exp(gs[b][last:last + 1, :]), (8, LANES)), x_ref[...])[0:1, :]
             for b in range(nb)]
    xss = [xc_ref[b, :, 0:SSD_W] for b in range(nb)]

    cms, bms, cbs = {}, {}, {}
    for b, grp in pairs:
        b0 = SSD_W + grp * SSD_STATE
        c0 = SSD_W + SSD_GROUPS * SSD_STATE + grp * SSD_STATE
        bms[b, grp] = xc_ref[b, :, b0:b0 + SSD_STATE]
        cms[b, grp] = xc_ref[b, :, c0:c0 + SSD_STATE].astype(BF16)
    for b, grp in pairs:
        cbs[b, grp] = _dot_nt(cms[b, grp], bms[b, grp].astype(BF16))

    a_cat, k_cat, x_bd, e_in = {}, {}, {}, {}
    for b, grp in pairs:
        bm_t = bms[b, grp].T
        xs_g = xss[b][:, grp * gw:(grp + 1) * gw].astype(BF16)
        a_parts, k_parts, x_parts, col_parts = [], [], [], []
        for hh in range(hpg):
            hd = grp * hpg + hh
            col = jnp.broadcast_to(gs[b][:, d0 + hd:d0 + hd + 1], (c, c))
            dec = jnp.where(tri_mask, jnp.exp(jnp.minimum(col - gt8s[b][hd:hd + 1, :], 0.0)), 0.0)
            a_parts.append((cbs[b, grp] * (dec * dt8s[b][hd:hd + 1, :])).astype(BF16))
            k_parts.append((bm_t * w8s[b][hd:hd + 1, :]).astype(BF16))
            x_parts.append(xs_g * hm_b[hh:hh + 1, :])
            col_parts.append(col)
        a_cat[b, grp] = jnp.concatenate(a_parts, axis=1)
        k_cat[b, grp] = jnp.concatenate(k_parts, axis=1)
        x_bd[b, grp] = jnp.concatenate(x_parts, axis=0)
        e_in[b, grp] = jnp.exp(jnp.concatenate(
            [jnp.where(lane_lo, col_parts[2 * t], col_parts[2 * t + 1]) for t in range(hpg // 2)], axis=1))

    sts = {p: st_ref[p[0], p[1]] for p in pairs}
    intra = {p: _dot(a_cat[p], x_bd[p]) for p in pairs}
    inter = {p: e_in[p] * _dot(cms[p], sts[p].astype(BF16)) for p in pairs}
    for b, grp in pairs:
        st_ref[b, grp] = (sts[b, grp] * gl_xs[b][:, grp * gw:(grp + 1) * gw]
                          + _dot(k_cat[b, grp], x_bd[b, grp]))

    for b in range(nb):
        xs = xss[b]
        o = jnp.concatenate([intra[b, grp] + inter[b, grp] for grp in range(SSD_GROUPS)], axis=1)
        if direction == 0:
            o_ref[b] = o
        else:
            y = (of_ref[b] + o + dsk_ref[...] * xs) * _silu(z_ref[b])
            o_ref[b] = _rms(y, nw_ref[...]).astype(BF16)


def _ssd_call(direction, xconv, small_d, p_z, o_f, dtb_d, alog_d, dsk, nw, consts, *, nctx):
    nb, rb, _ = xconv.shape
    c = SSD_CHUNK
    assert c == LANES
    nch, ncc = rb // c, nctx // c
    kern = functools.partial(_ssd_kernel, direction=direction, nb=nb, c=c)
    chunk = lambda n: (0, _chunk_of(direction, n, ncc, nch), 0)
    const2 = lambda n: (0, 0)
    blk = lambda w: pl.BlockSpec((nb, c, w), chunk)
    full = lambda a: pl.BlockSpec(a.shape, const2)
    ins = [xconv, small_d]
    specs = [blk(SSD_CONV_CH), blk(LANES)]
    if direction == 1:
        ins += [p_z, o_f]
        specs += [blk(SSD_W), blk(SSD_W)]
    tail = [dtb_d, alog_d, consts["tri_ssd"][direction], consts["ssd_x"], consts["hmask"]]
    if direction == 1:
        tail += [dsk, nw]
    ins += tail
    specs += [full(a) for a in tail]
    return pl.pallas_call(
        kern,
        grid=(nch,),
        in_specs=specs,
        out_specs=blk(SSD_W),
        out_shape=jax.ShapeDtypeStruct((nb, rb, SSD_W), F32 if direction == 0 else BF16),
        scratch_shapes=[pltpu.VMEM((nb, SSD_GROUPS, SSD_STATE, 256), F32)],
        compiler_params=_scan_params(),
        name="ssd_fwd" if direction == 0 else "ssd_bwd",
    )(*ins)


def _ret_log_gamma_row(width):
    lane = lax.broadcasted_iota(jnp.int32, (1, width), 1)
    row = jnp.zeros((1, width), F32)
    for h in range(RET_HEADS):
        lg = math.log1p(-(2.0 ** (-5.0 - h)))
        row = jnp.where(lane // RET_HEAD_DIM == h, lg, row)
    return row


def _rope(t, cos, sin_signed):
    lane = lax.broadcasted_iota(jnp.int32, t.shape, 1)
    half = RET_HEAD_DIM // 2
    swapped = jnp.where(lane % RET_HEAD_DIM < half,
                        pltpu.roll(t, LANES - half, 1), pltpu.roll(t, half, 1))
    return t * cos + swapped * sin_signed


def _ret_kernel(*refs, direction, nb, c):
    if direction == 0:
        (p_ref, cos_ref, sin_ref, avg_ref, hm_ref, o_ref, st_ref) = refs
    else:
        (p_ref, cos_ref, sin_ref, of_ref, avg_ref, hm_ref, nw_ref, o_ref, st_ref) = refs
    w = RET_W

    @pl.when(pl.program_id(0) == 0)
    def _():
        st_ref[...] = jnp.zeros_like(st_ref)

    cos = cos_ref[...]
    sin = sin_ref[...]
    lg_row = _ret_log_gamma_row(w)
    pos = lax.broadcasted_iota(jnp.int32, (c, 1), 0)
    cnt_in = (pos + 1 if direction == 0 else c - pos).astype(F32)
    cnt_st = (c - 1 - pos if direction == 0 else pos).astype(F32)
    e_in = jnp.exp(cnt_in * lg_row)
    e_st = jnp.exp(cnt_st * lg_row)
    e_all = jnp.exp(float(c) * lg_row)
    ii = lax.broadcasted_iota(jnp.int32, (c, c), 0)
    jj = lax.broadcasted_iota(jnp.int32, (c, c), 1)
    dist = (ii - jj) if direction == 0 else (jj - ii)
    dec_parts = []
    for h in range(RET_HEADS):
        lg = math.log1p(-(2.0 ** (-5.0 - h)))
        dec_parts.append(jnp.where(dist >= 0, jnp.exp(dist.astype(F32) * lg), 0.0))
    dec = jnp.concatenate(dec_parts, axis=1)
    bd_mask = avg_ref[...].astype(F32)

    hm_b = hm_ref[...].astype(BF16)
    qbs, kbs, vbs, kes = [], [], [], []
    for b in range(nb):
        q_halves, k_halves = [], []
        for hf in range(w // LANES):
            qh = p_ref[b, :, hf * LANES:(hf + 1) * LANES] * (RET_HEAD_DIM ** -0.5)
            kh = p_ref[b, :, w + hf * LANES:w + (hf + 1) * LANES]
            q_halves.append(_rope(qh, cos, sin))
            k_halves.append(_rope(kh, cos, sin))
        k = jnp.concatenate(k_halves, axis=1)
        qbs.append(jnp.concatenate(q_halves, axis=1).astype(BF16))
        kbs.append(k.astype(BF16))
        kes.append((k * e_st).astype(BF16))
        vbs.append(p_ref[b, :, 2 * w:3 * w].astype(BF16))
    k_bd = [jnp.concatenate([kbs[b] * hm_b[h:h + 1, :] for h in range(RET_HEADS)], axis=0) for b in range(nb)]
    v_bd = [jnp.concatenate([vbs[b] * hm_b[h:h + 1, :] for h in range(RET_HEADS)], axis=0) for b in range(nb)]
    scores = [(_dot_nt(qbs[b], k_bd[b]) * dec).astype(BF16) for b in range(nb)]
    intra = [_dot(scores[b], v_bd[b]) for b in range(nb)]
    sts = [st_ref[b] for b in range(nb)]
    outs = [intra[b] + e_in * _dot(qbs[b], sts[b].astype(BF16)) for b in range(nb)]
    for b in range(nb):
        st_ref[b] = sts[b] * e_all + _dot_tn(kes[b], vbs[b]) * bd_mask
    if direction == 0:
        for b in range(nb):
            o_ref[b] = outs[b]
    else:
        tots = [of_ref[b] + outs[b] for b in range(nb)]
        xcs = [tots[b] - _dot_sel_lhs(tots[b], avg_ref[...], 2) * (1.0 / RET_HEAD_DIM) for b in range(nb)]
        vrs = [_dot_sel_lhs(xcs[b] * xcs[b], avg_ref[...], 2) * (1.0 / RET_HEAD_DIM) for b in range(nb)]
        for b in range(nb):
            y = xcs[b] * lax.rsqrt(vrs[b] + RMS_EPS) * nw_ref[...]
            o_ref[b] = (y * _silu(p_ref[b, :, 3 * w:4 * w])).astype(BF16)


def _ret_call(direction, p_ret, cos_t, sin_t, o_f, nw, consts, *, nctx):
    nb, rb, _ = p_ret.shape
    c = RET_CHUNK
    nch, ncc = rb // c, nctx // c
    kern = functools.partial(_ret_kernel, direction=direction, nb=nb, c=c)
    chunk = lambda n: (0, _chunk_of(direction, n, ncc, nch), 0)
    tchunk = lambda n: (_chunk_of(direction, n, ncc, nch), 0)
    const2 = lambda n: (0, 0)
    blk = lambda w: pl.BlockSpec((nb, c, w), chunk)
    full = lambda a: pl.BlockSpec(a.shape, const2)
    ins = [p_ret, cos_t, sin_t]
    specs = [blk(4 * RET_W), pl.BlockSpec((c, LANES), tchunk), pl.BlockSpec((c, LANES), tchunk)]
    if direction == 1:
        ins.append(o_f)
        specs.append(blk(RET_W))
    tail = [consts["avg256"], consts["hmask"]]
    if direction == 1:
        tail.append(nw)
    ins += tail
    specs += [full(a) for a in tail]
    return pl.pallas_call(
        kern,
        grid=(nch,),
        in_specs=specs,
        out_specs=blk(RET_W),
        out_shape=jax.ShapeDtypeStruct((nb, rb, RET_W), F32 if direction == 0 else BF16),
        scratch_shapes=[pltpu.VMEM((nb, RET_W, RET_W), F32)],
        compiler_params=_scan_params(),
        name="ret_fwd" if direction == 0 else "ret_bwd",
    )(*ins)


def _outproj_kernel(h_ref, mod_ref, gla_ref, ssd_ref, ret_ref, wo_ref, nw_ref, o_ref, *, tm, spb, ncs, nb):
    i = pl.program_id(0)
    d = D_MODEL
    mixed = jnp.concatenate([gla_ref[...], ssd_ref[...], ret_ref[...]], axis=1)
    m = _dot(mixed, wo_ref[...])
    for k in range(tm // SLAB):
        mrow = _mod_row(i * (tm // SLAB) + k, spb, ncs, nb)
        gate = mod_ref[mrow][:, 2 * d:3 * d]
        sl = slice(k * SLAB, (k + 1) * SLAB)
        o_ref[sl, :] = h_ref[sl, :] + gate * _rms(m[sl, :], nw_ref[...])


def _outproj_call(h, mod, gla, ssd, ret, wo, nw, *, spb, ncs, nb):
    r, d = h.shape
    tm = TM_DENSE
    kern = functools.partial(_outproj_kernel, tm=tm, spb=spb, ncs=ncs, nb=nb)
    row = lambda i: (i, 0)
    const = lambda i: (0, 0)
    return pl.pallas_call(
        kern,
        grid=(r // tm,),
        in_specs=[
            pl.BlockSpec((tm, d), row),
            pl.BlockSpec(mod.shape, lambda i: (0, 0, 0)),
            pl.BlockSpec((tm, GLA_W), row),
            pl.BlockSpec((tm, SSD_W), row),
            pl.BlockSpec((tm, RET_W), row),
            pl.BlockSpec(wo.shape, const),
            pl.BlockSpec((1, d), const),
        ],
        out_specs=pl.BlockSpec((tm, d), row),
        out_shape=jax.ShapeDtypeStruct((r, d), F32),
        compiler_params=pltpu.CompilerParams(
            dimension_semantics=("arbitrary",), vmem_limit_bytes=VMEM_LIMIT),
        name="out_proj",
    )(h, mod, gla, ssd, ret, wo, nw)


def _ffn_kernel(h_ref, mod_ref, npre_ref, w13_ref, w2_ref, npost_ref, o_ref, hb_ref, acc_ref,
                *, tm, spb, ncs, nb):
    i = pl.program_id(0)
    d = D_MODEL
    for k in range(tm // SLAB):
        mrow = _mod_row(i * (tm // SLAB) + k, spb, ncs, nb)
        shift = mod_ref[mrow][:, 3 * d:4 * d]
        scale = mod_ref[mrow][:, 4 * d:5 * d]
        sl = slice(k * SLAB, (k + 1) * SLAB)
        y = _rms(h_ref[sl, :], npre_ref[...]) * (1.0 + scale) + shift
        hb_ref[sl, :] = y.astype(BF16)
    hb = hb_ref[...]
    nck = FFN_HIDDEN // FFN_CHUNK

    def gate_up(cidx):
        lo = cidx * FFN_CHUNK
        gate = _dot(hb, w13_ref[:, lo:lo + FFN_CHUNK])
        up = _dot(hb, w13_ref[:, FFN_HIDDEN + lo:FFN_HIDDEN + lo + FFN_CHUNK])
        return (_silu(gate) * up).astype(BF16)

    act = gate_up(0)
    for cidx in range(nck):
        nxt = gate_up(cidx + 1) if cidx + 1 < nck else None
        part = _dot(act, w2_ref[cidx * FFN_CHUNK:(cidx + 1) * FFN_CHUNK, :])
        if cidx == 0:
            acc_ref[...] = part
        else:
            acc_ref[...] += part
        act = nxt
    for k in range(tm // SLAB):
        mrow = _mod_row(i * (tm // SLAB) + k, spb, ncs, nb)
        gate = mod_ref[mrow][:, 5 * d:6 * d]
        sl = slice(k * SLAB, (k + 1) * SLAB)
        o_ref[sl, :] = h_ref[sl, :] + gate * _rms(acc_ref[sl, :], npost_ref[...])


def _ffn_call(h, mod, npre, w13, w2, npost, *, spb, ncs, nb):
    r, d = h.shape
    tm = TM_DENSE
    kern = functools.partial(_ffn_kernel, tm=tm, spb=spb, ncs=ncs, nb=nb)
    row = lambda i: (i, 0)
    const = lambda i: (0, 0)
    return pl.pallas_call(
        kern,
        grid=(r // tm,),
        in_specs=[
            pl.BlockSpec((tm, d), row),
            pl.BlockSpec(mod.shape, lambda i: (0, 0, 0)),
            pl.BlockSpec((1, d), const),
            pl.BlockSpec(w13.shape, const),
            pl.BlockSpec(w2.shape, const),
            pl.BlockSpec((1, d), const),
        ],
        out_specs=pl.BlockSpec((tm, d), row),
        out_shape=jax.ShapeDtypeStruct((r, d), F32),
        scratch_shapes=[pltpu.VMEM((tm, d), BF16), pltpu.VMEM((tm, d), F32)],
        compiler_params=pltpu.CompilerParams(
            dimension_semantics=("arbitrary",), vmem_limit_bytes=VMEM_LIMIT),
        name="ffn",
    )(h, mod, npre, w13, w2, npost)


def _tri_pair(c):
    i = jnp.arange(c)[:, None]
    j = jnp.arange(c)[None, :]
    return jnp.stack([(j <= i), (j >= i)]).astype(BF16)


def _gla_cum_pair(cg, sb):
    i = jnp.arange(cg)[:, None]
    j = jnp.arange(cg)[None, :]
    same = (i // sb) == (j // sb)
    fwd = jnp.concatenate([same & (j <= i), same], axis=0)
    bwd = jnp.concatenate([same & (j >= i), same], axis=0)
    return jnp.stack([fwd, bwd]).astype(BF16)


def _constants():
    lane256 = jnp.arange(256)
    consts = {
        "gla_cum": _gla_cum_pair(GLA_BLOCK, GLA_SUB),
        "tri_ssd": _tri_pair(SSD_CHUNK),
        "avg256": (lane256[:, None] // 64 == lane256[None, :] // 64).astype(BF16),
        "gla_e2": (jnp.arange(GLA_KDIM)[:, None] // GLA_HEAD_K == lane256[None, :] // GLA_HEAD_V).astype(BF16),
        "gla_bdm": (lane256[:, None] // GLA_HEAD_V == jnp.arange(GLA_KDIM)[None, :] // GLA_HEAD_K).astype(F32),
        "ssd_x": (jnp.arange(LANES)[:, None] - SMALL_DT_LANE
                  == jnp.arange(SSD_W)[None, :] // SSD_HEAD_DIM).astype(BF16),
        "hmask": (jnp.arange(8)[:, None] == lane256[None, :] // 64).astype(F32),
    }
    return consts


def _rope_tables(t_lat, nctx):
    rows = t_lat // GRID_W
    row = jnp.repeat(jnp.arange(rows), GRID_W).astype(F32)
    col = jnp.tile(jnp.arange(GRID_W), rows).astype(F32)
    n_freq = RET_HEAD_DIM // 4
    inv_freq = ROPE_BASE ** (-jnp.arange(n_freq, dtype=F32) / n_freq)
    ang = jnp.concatenate([row[:, None] * inv_freq, col[:, None] * inv_freq], axis=-1)
    cos = jnp.cos(ang)
    sin = jnp.sin(ang)
    cos_t = jnp.concatenate([cos, cos, cos, cos], axis=-1)
    sin_t = jnp.concatenate([-sin, sin, -sin, sin], axis=-1)
    cos_t = jnp.concatenate([jnp.ones((nctx, LANES), F32), cos_t], axis=0)
    sin_t = jnp.concatenate([jnp.zeros((nctx, LANES), F32), sin_t], axis=0)
    return cos_t, sin_t


def kernel(x, c, ctx, c_ctx, ada_w, ada_b, norm_mix_pre, norm_mix_post, norm_ffn_pre, norm_ffn_post,
           w_in, w_out, gla_gate_up, gla_gate_b, gla_norm, ssd_conv_w, ssd_conv_b, ssd_dt_bias,
           ssd_a_log, ssd_d, ssd_norm, ret_norm, ffn_w13, ffn_w2):
    nb, t_lat, d = x.shape
    nctx = ctx.shape[1]
    depth = ada_w.shape[0]
    rb = nctx + t_lat
    r = nb * rb
    spb = rb // SLAB
    ncs = nctx // SLAB
    assert d == D_MODEL and nb + 1 <= 8
    assert nctx % SLAB == 0 and t_lat % SLAB == 0 and r % TM_DENSE == 0

    consts = _constants()
    cos_t, sin_t = _rope_tables(t_lat, nctx)

    cvec = jnp.zeros((8, d), F32).at[:nb].set(c).at[nb].set(c_ctx)
    mods = _ada_call(cvec, ada_w, ada_b)

    s1, s2 = GLA_COLS, GLA_COLS + SSD_COLS
    lr0 = 2 * GLA_KDIM + 2 * GLA_W
    dt0 = s1 + SSD_W + SSD_CONV_CH
    wm = jnp.concatenate([w_in[:, :, 0:lr0], w_in[:, :, s1:dt0], w_in[:, :, s2:]], axis=-1).astype(BF16)
    zpad = jnp.zeros((depth, d, LANES - GLA_RANK - SSD_HEADS), F32)
    ws = jnp.concatenate(
        [w_in[:, :, lr0:lr0 + GLA_RANK], w_in[:, :, dt0:dt0 + SSD_HEADS], zpad,
         w_in[:, :, lr0 + GLA_RANK:lr0 + 2 * GLA_RANK], w_in[:, :, dt0 + SSD_HEADS:dt0 + 2 * SSD_HEADS], zpad],
        axis=-1).astype(BF16)
    wo = w_out.astype(BF16)
    w13 = ffn_w13.astype(BF16)
    w2 = ffn_w2.astype(BF16)

    wg = jnp.zeros((depth, 2, LANES, GLA_KDIM), F32).at[:, :, :GLA_RANK, :].set(gla_gate_up)
    gb = gla_gate_b.reshape(depth, 2, 1, GLA_KDIM)
    lane_dt = slice(SMALL_DT_LANE, SMALL_DT_LANE + SSD_HEADS)
    dtb = jnp.zeros((depth, 2, 1, LANES), F32).at[:, :, 0, lane_dt].set(ssd_dt_bias)
    alog = jnp.zeros((depth, 2, 1, LANES), F32).at[:, :, 0, lane_dt].set(ssd_a_log)
    dsk = jnp.repeat(ssd_d, SSD_HEAD_DIM, axis=-1).reshape(depth, 1, SSD_W)
    conv_w = jnp.zeros((depth, 8, SSD_CONV_CH), F32).at[:, :SSD_CONV_W].set(ssd_conv_w)

    h = jnp.concatenate([ctx, x], axis=1).reshape(r, d)
    meta = dict(spb=spb, ncs=ncs, nb=nb)
    row1 = lambda a: a.reshape(1, -1)
    for l in range(depth):
        mod = mods[l].reshape(8, 1, 6 * d)
        p_gla, p_z, xconv, p_ret, small = _inproj_call(
            h, mod, row1(norm_mix_pre[l]), wm[l], ws[l], conv_w[l], row1(ssd_conv_b[l]), **meta)
        small = small.reshape(2, nb, rb, LANES)
        xconv = xconv.reshape(nb, rb, SSD_CONV_CH)
        p_gla = p_gla.reshape(nb, rb, 768)
        p_z = p_z.reshape(nb, rb, SSD_W)
        p_ret = p_ret.reshape(nb, rb, 4 * RET_W)
        gla_f = _gla_call(0, p_gla, small[0], None, wg[l, 0], gb[l, 0], None, consts, nctx=nctx)
        gla = _gla_call(1, p_gla, small[1], gla_f, wg[l, 1], gb[l, 1], row1(gla_norm[l]), consts, nctx=nctx)
        ssd_f = _ssd_call(0, xconv, small[0], None, None, dtb[l, 0], alog[l, 0], None, None, consts, nctx=nctx)
        ssd = _ssd_call(1, xconv, small[1], p_z, ssd_f, dtb[l, 1], alog[l, 1], dsk[l], row1(ssd_norm[l]),
                        consts, nctx=nctx)
        ret_f = _ret_call(0, p_ret, cos_t, sin_t, None, None, consts, nctx=nctx)
        ret = _ret_call(1, p_ret, cos_t, sin_t, ret_f, row1(ret_norm[l]), consts, nctx=nctx)
        h = _outproj_call(h, mod, gla.reshape(r, GLA_W), ssd.reshape(r, SSD_W), ret.reshape(r, RET_W),
                          wo[l], row1(norm_mix_post[l]), **meta)
        h = _ffn_call(h, mod, row1(norm_ffn_pre[l]), w13[l], w2[l], row1(norm_ffn_post[l]), **meta)
    return h.reshape(nb, rb, d)[:, nctx:, :]
```

```python
import functools
import math

import jax
import jax.numpy as jnp
from jax import lax
from jax.experimental import pallas as pl
from jax.experimental.pallas import tpu as pltpu

F32 = jnp.float32
BF16 = jnp.bfloat16

D_MODEL = 1024
GRID_W = 64
RMS_EPS = 1e-6
GLA_W = 256
GLA_HEADS = 4
GLA_HEAD_V = 64
GLA_HEAD_K = 32
GLA_KDIM = 128
GLA_RANK = 16
GLA_GATE_TAU = 16.0
SSD_W = 512
SSD_HEADS = 8
SSD_HEAD_DIM = 64
SSD_GROUPS = 2
SSD_STATE = 128
SSD_CONV_W = 5
SSD_CONV_CH = 1024
RET_W = 256
RET_HEADS = 4
RET_HEAD_DIM = 64
ROPE_BASE = 10000.0
GLA_COLS = 800
SSD_COLS = 1552
FFN_HIDDEN = 2816

LANES = 128
SLAB = 256
TM_DENSE = 512
FFN_CHUNK = 256
GLA_SUB = 16
GLA_BLOCK = 256
SSD_CHUNK = 128
RET_CHUNK = 128
SMALL_DT_LANE = 16
VMEM_LIMIT = 56 * 1024 * 1024


def _silu(x):
    return x * jax.nn.sigmoid(x)


def _dot(a, b):
    return jnp.dot(a, b, preferred_element_type=F32)


def _dot_nt(a, b):
    return lax.dot_general(a, b, (((1,), (1,)), ((), ())), preferred_element_type=F32)


def _dot_tn(a, b):
    return lax.dot_general(a, b, (((0,), (0,)), ((), ())), preferred_element_type=F32)


def _split_bf16(x, terms):
    parts = []
    r = x
    for _ in range(terms):
        hi = r.astype(BF16)
        parts.append(hi)
        r = r - hi.astype(F32)
    return parts


def _dot_sel_rhs(m01, x, terms=3):
    acc = None
    for p in _split_bf16(x, terms):
        t = _dot(m01, p)
        acc = t if acc is None else acc + t
    return acc


def _dot_sel_lhs(x, m01, terms=3):
    acc = None
    for p in _split_bf16(x, terms):
        t = _dot(p, m01)
        acc = t if acc is None else acc + t
    return acc


def _dot3(a, b):
    a_hi = a.astype(BF16)
    a_lo = (a - a_hi.astype(F32)).astype(BF16)
    b_hi = b.astype(BF16)
    b_lo = (b - b_hi.astype(F32)).astype(BF16)
    return _dot(a_hi, b_hi) + _dot(a_lo, b_hi) + _dot(a_hi, b_lo)


def _mod_row(slab, spb, ncs, nb):
    return jnp.where(slab % spb < ncs, nb, slab // spb)


def _rms(x, w):
    return x * lax.rsqrt(jnp.mean(x * x, axis=-1, keepdims=True) + RMS_EPS) * w


def _ada_kernel(cv_ref, w_ref, b_ref, o_ref):
    s = _silu(cv_ref[...]).astype(BF16)
    o_ref[...] = _dot(s, w_ref[...].astype(BF16)) + b_ref[...]


def _ada_call(cvec, ada_w, ada_b):
    depth, d, n6 = ada_w.shape
    tn = 2048
    return pl.pallas_call(
        _ada_kernel,
        grid=(depth, n6 // tn),
        in_specs=[
            pl.BlockSpec((8, d), lambda l, j: (0, 0)),
            pl.BlockSpec((None, d, tn), lambda l, j: (l, 0, j)),
            pl.BlockSpec((None, 1, tn), lambda l, j: (l, 0, j)),
        ],
        out_specs=pl.BlockSpec((None, 8, tn), lambda l, j: (l, 0, j)),
        out_shape=jax.ShapeDtypeStruct((depth, 8, n6), F32),
        compiler_params=pltpu.CompilerParams(
            dimension_semantics=("arbitrary", "arbitrary"), vmem_limit_bytes=VMEM_LIMIT),
        name="ada_mod",
    )(cvec, ada_w, ada_b.reshape(depth, 1, n6))


def _inproj_kernel(h_ref, hp_ref, hn_ref, mod_ref, nw_ref, wm_ref, ws_ref, cw_ref, cb_ref,
                   gla_ref, z_ref, xc_ref, ret_ref, smf_ref, smb_ref, yb_ref, scr_ref, *, tm, spb, ncs, nb):
    i = pl.program_id(0)
    d = D_MODEL
    nsl = tm // SLAB
    x0, x1 = 1280, 2304

    def normed(x, mrow):
        return _rms(x, nw_ref[...]) * (1.0 + mod_ref[mrow][:, d:2 * d]) + mod_ref[mrow][:, 0:d]

    mrows = [_mod_row(i * nsl + k, spb, ncs, nb) for k in range(nsl)]
    for k in range(nsl):
        yb_ref[k * SLAB:(k + 1) * SLAB, :] = normed(h_ref[k * SLAB:(k + 1) * SLAB, :], mrows[k]).astype(BF16)
    yb = yb_ref[...]

    yh = jnp.concatenate([normed(hp_ref[...], mrows[0]), normed(hn_ref[...], mrows[nsl - 1])], axis=0)
    halo = _dot(yh.astype(BF16), wm_ref[:, x0:x1])
    xbc = _dot(yb, wm_ref[:, x0:x1])
    gla_ref[...] = _dot(yb, wm_ref[:, 0:768])
    z_ref[...] = _dot(yb, wm_ref[:, 768:x0])
    ret_ref[...] = _dot(yb, wm_ref[:, x1:3328])
    smf_ref[...] = _dot(yb, ws_ref[:, 0:LANES])
    smb_ref[...] = _dot(yb, ws_ref[:, LANES:2 * LANES])
    pad = (SSD_CONV_W - 1) // 2
    for k in range(nsl):
        pos = (i * nsl + k) % spb
        first = jnp.logical_or(pos == 0, pos == ncs)
        last = jnp.logical_or(pos == ncs - 1, pos == spb - 1)
        before = halo[0:8] if k == 0 else xbc[k * SLAB - 8:k * SLAB, :]
        after = halo[8:16] if k == nsl - 1 else xbc[(k + 1) * SLAB:(k + 1) * SLAB + 8, :]
        scr_ref[0:8, :] = jnp.where(first, 0.0, before)
        scr_ref[8:8 + SLAB, :] = xbc[k * SLAB:(k + 1) * SLAB, :]
        scr_ref[8 + SLAB:16 + SLAB, :] = jnp.where(last, 0.0, after)
        acc = cb_ref[...] + cw_ref[0:1, :] * scr_ref[pl.ds(8 - pad, SLAB), :]
        for t in range(1, SSD_CONV_W):
            acc = acc + cw_ref[t:t + 1, :] * scr_ref[pl.ds(8 - pad + t, SLAB), :]
        xc_ref[k * SLAB:(k + 1) * SLAB, :] = _silu(acc)


def _layer_spec(w, layer):
    zeros = (0,) * (w.ndim - 1)
    return pl.BlockSpec((None,) + w.shape[1:], lambda i: (layer,) + zeros)


def _inproj_call(h, mod, nw, wm, ws, cw, cb, *, layer, spb, ncs, nb):
    r, d = h.shape
    tm = TM_DENSE
    kern = functools.partial(_inproj_kernel, tm=tm, spb=spb, ncs=ncs, nb=nb)
    row = lambda i: (i, 0)
    const = lambda i: (0, 0)
    per = tm // 8
    return pl.pallas_call(
        kern,
        grid=(r // tm,),
        in_specs=[
            pl.BlockSpec((tm, d), row),
            pl.BlockSpec((8, d), lambda i: (jnp.maximum(i * per - 1, 0), 0)),
            pl.BlockSpec((8, d), lambda i: (jnp.minimum((i + 1) * per, r // 8 - 1), 0)),
            pl.BlockSpec(mod.shape, lambda i: (0, 0, 0)),
            pl.BlockSpec((1, d), const),
            _layer_spec(wm, layer),
            _layer_spec(ws, layer),
            _layer_spec(cw, layer),
            pl.BlockSpec(cb.shape, const),
        ],
        out_specs=[
            pl.BlockSpec((tm, 768), row),
            pl.BlockSpec((tm, 512), row),
            pl.BlockSpec((tm, 1024), row),
            pl.BlockSpec((tm, 1024), row),
            pl.BlockSpec((tm, LANES), row),
            pl.BlockSpec((tm, LANES), row),
        ],
        out_shape=[
            jax.ShapeDtypeStruct((r, 768), F32),
            jax.ShapeDtypeStruct((r, 512), F32),
            jax.ShapeDtypeStruct((r, 1024), F32),
            jax.ShapeDtypeStruct((r, 1024), F32),
            jax.ShapeDtypeStruct((r, LANES), F32),
            jax.ShapeDtypeStruct((r, LANES), F32),
        ],
        scratch_shapes=[pltpu.VMEM((tm, d), BF16), pltpu.VMEM((SLAB + 16, SSD_CONV_CH), F32)],
        compiler_params=pltpu.CompilerParams(
            dimension_semantics=("arbitrary",), vmem_limit_bytes=VMEM_LIMIT),
        name="in_proj",
    )(h, h, h, mod, nw, wm, ws, cw, cb)


def _chunk_of(direction, n, ncc, nch):
    if direction == 0:
        return n
    return jnp.where(n < ncc, ncc - 1 - n, nch + ncc - 1 - n)


def _scan_params():
    return pltpu.CompilerParams(dimension_semantics=("arbitrary",), vmem_limit_bytes=VMEM_LIMIT)


def _gla_kernel(*refs, direction, nb, cg):
    if direction == 0:
        (p_ref, sm_ref, wg_ref, gb_ref, cum_ref, e2_ref, bdm_ref, o_ref, st_ref) = refs
    else:
        (p_ref, sm_ref, of_ref, wg_ref, gb_ref, cum_ref, e2_ref, bdm_ref, avg_ref, nw_ref,
         o_ref, st_ref) = refs
    sb = GLA_SUB
    nsb = cg // sb

    @pl.when(pl.program_id(0) == 0)
    def _():
        st_ref[...] = jnp.zeros_like(st_ref)

    half = 8
    rows8 = lax.broadcasted_iota(jnp.int32, (half, GLA_KDIM), 0)
    masks8 = [(rows8 >= j) if direction == 0 else (rows8 <= j) for j in range(half)]

    def half_state(hf, j):
        if hf == j // half:
            return "part"
        kept = hf > j // half if direction == 0 else hf < j // half
        return "full" if kept else "zero"

    pre = []
    for b in range(nb):
        z = _dot3(sm_ref[b], wg_ref[...]) + gb_ref[...]
        logg = (jnp.minimum(z, 0.0) - jnp.log1p(jnp.exp(-jnp.abs(z)))) * (1.0 / GLA_GATE_TAU)
        cums = _dot_sel_rhs(cum_ref[...], logg)
        gs_all = cums[0:cg]
        gl = cums[cg:2 * cg]
        qs_all = p_ref[b, :, 0:GLA_KDIM] * (GLA_HEAD_K ** -0.5)
        k_all = p_ref[b, :, GLA_KDIM:2 * GLA_KDIM]
        v_all = p_ref[b, :, 2 * GLA_KDIM:2 * GLA_KDIM + GLA_W]
        pre.append(dict(
            gs=gs_all, egl=jnp.exp(gl), qs=qs_all, k=k_all, v=v_all,
            qt=(qs_all * jnp.exp(gs_all)).astype(BF16),
            kk=(k_all * jnp.exp(gl - gs_all)).astype(BF16),
            vb=v_all.astype(BF16)))

    sts = [st_ref[b] for b in range(nb)]
    o_parts = [[None] * nsb for _ in range(nb)]
    for it in range(nsb):
        a = it if direction == 0 else nsb - 1 - it
        sl = slice(a * sb, (a + 1) * sb)
        for b in range(nb):
            pb = pre[b]
            gs, qs, kb, vb = pb["gs"][sl], pb["qs"][sl], pb["k"][sl], pb["v"][sl]
            inter = _dot_nt(pb["qt"][sl], sts[b].astype(BF16))
            parts = []
            for j in range(sb):
                halves = []
                for hf in range(sb // half):
                    rs = slice(hf * half, (hf + 1) * half)
                    if half_state(hf, j) == "zero":
                        halves.append(jnp.zeros((half, GLA_KDIM), F32))
                        continue
                    dec = jnp.exp(gs[rs] - gs[j:j + 1, :])
                    if half_state(hf, j) == "part":
                        dec = jnp.where(masks8[j % half], dec, 0.0)
                    halves.append(dec * qs[rs] * kb[j:j + 1, :])
                parts.append(jnp.concatenate(halves, axis=0).astype(BF16))
            sx = _dot(jnp.concatenate(parts, axis=0), e2_ref[...])
            acc = [None] * (sb // half)
            for j in range(sb):
                for hf in range(sb // half):
                    if half_state(hf, j) == "zero":
                        continue
                    term = sx[j * sb + hf * half:j * sb + (hf + 1) * half] * vb[j:j + 1, :]
                    acc[hf] = term if acc[hf] is None else acc[hf] + term
            o_parts[b][a] = jnp.concatenate(acc, axis=0) + inter
            sts[b] = (sts[b] * pb["egl"][a * sb:a * sb + 1, :]
                      + _dot_tn(pb["vb"][sl], pb["kk"][sl]) * bdm_ref[...])
    for b in range(nb):
        st_ref[b] = sts[b]
    outs = [jnp.concatenate(o_parts[b], axis=0) for b in range(nb)]
    if direction == 0:
        for b in range(nb):
            o_ref[b] = outs[b]
    else:
        tots = [of_ref[b] + outs[b] for b in range(nb)]
        mss = [_dot_sel_lhs(tots[b] * tots[b], avg_ref[...], 2) * (1.0 / GLA_HEAD_V) for b in range(nb)]
        for b in range(nb):
            y = tots[b] * lax.rsqrt(mss[b] + RMS_EPS) * nw_ref[...]
            r = p_ref[b, :, 2 * GLA_KDIM + GLA_W:2 * GLA_KDIM + 2 * GLA_W]
            o_ref[b] = (y * _silu(r)).astype(BF16)


def _gla_call(direction, p_gla, small_d, o_f, wg_d, gb_d, nw, consts, *, nctx):
    nb, rb, _ = p_gla.shape
    cg = GLA_BLOCK
    nch, ncc = rb // cg, nctx // cg
    kern = functools.partial(_gla_kernel, direction=direction, nb=nb, cg=cg)
    chunk = lambda n: (0, _chunk_of(direction, n, ncc, nch), 0)
    const2 = lambda n: (0, 0)
    blk = lambda w: pl.BlockSpec((nb, cg, w), chunk)
    full = lambda a: pl.BlockSpec(a.shape, const2)
    cum = consts["gla_cum"][direction]
    ins = [p_gla, small_d]
    specs = [blk(768), blk(LANES)]
    if direction == 1:
        ins.append(o_f)
        specs.append(blk(GLA_W))
    tail = [wg_d, gb_d, cum, consts["gla_e2"], consts["gla_bdm"]]
    if direction == 1:
        tail += [consts["avg256"], nw]
    ins += tail
    specs += [full(a) for a in tail]
    scratch = [pltpu.VMEM((nb, GLA_W, GLA_KDIM), F32)]
    return pl.pallas_call(
        kern,
        grid=(nch,),
        in_specs=specs,
        out_specs=blk(GLA_W),
        out_shape=jax.ShapeDtypeStruct((nb, rb, GLA_W), F32 if direction == 0 else BF16),
        scratch_shapes=scratch,
        compiler_params=_scan_params(),
        name="gla_fwd" if direction == 0 else "gla_bwd",
    )(*ins)


def _ssd_kernel(*refs, direction, nb, c):
    if direction == 0:
        (xc_ref, sm_ref, dtb_ref, alog_ref, tri_ref, x_ref, hm_ref, o_ref, st_ref) = refs
    else:
        (xc_ref, sm_ref, z_ref, of_ref, dtb_ref, alog_ref, tri_ref, x_ref, hm_ref, dsk_ref, nw_ref,
         o_ref, st_ref) = refs
    hpg = SSD_HEADS // SSD_GROUPS
    gw = hpg * SSD_HEAD_DIM

    @pl.when(pl.program_id(0) == 0)
    def _():
        st_ref[...] = jnp.zeros_like(st_ref)

    ii = lax.broadcasted_iota(jnp.int32, (c, c), 0)
    jj = lax.broadcasted_iota(jnp.int32, (c, c), 1)
    tri_mask = (ii >= jj) if direction == 0 else (ii <= jj)
    neg_a = -jnp.exp(alog_ref[...])

    lane_lo = lax.broadcasted_iota(jnp.int32, (c, LANES), 1) < SSD_HEAD_DIM
    hm_b = hm_ref[...].astype(BF16)
    last = c - 1 if direction == 0 else 0
    d0, d1 = SMALL_DT_LANE, SMALL_DT_LANE + SSD_HEADS

    pairs = [(b, grp) for b in range(nb) for grp in range(SSD_GROUPS)]
    dts = []
    for b in range(nb):
        raw = sm_ref[b] + dtb_ref[...]
        dts.append(jnp.maximum(raw, 0.0) + jnp.log1p(jnp.exp(-jnp.abs(raw))))
    gs = [_dot_sel_rhs(tri_ref[...], dts[b] * neg_a) for b in range(nb)]
    gt8s = [gs[b].T[d0:d1, :] for b in range(nb)]
    dt8s = [dts[b].T[d0:d1, :] for b in range(nb)]
    w8s = [dt8s[b] * jnp.exp(gt8s[b][:, last:last + 1] - gt8s[b]) for b in range(nb)]
    gl_xs = [_dot_sel_lhs(jnp.broadcast_to(jnp.exp(gs[b][last:last + 1, :]), (8, LANES)), x_ref[...])[0:1, :]
             for b in range(nb)]
    xss = [xc_ref[b, :, 0:SSD_W] for b in range(nb)]

    cms, bms, cbs = {}, {}, {}
    for b, grp in pairs:
        b0 = SSD_W + grp * SSD_STATE
        c0 = SSD_W + SSD_GROUPS * SSD_STATE + grp * SSD_STATE
        bms[b, grp] = xc_ref[b, :, b0:b0 + SSD_STATE]
        cms[b, grp] = xc_ref[b, :, c0:c0 + SSD_STATE].astype(BF16)
    for b, grp in pairs:
        cbs[b, grp] = _dot_nt(cms[b, grp], bms[b, grp].astype(BF16))

    a_cat, k_cat, x_bd, e_in = {}, {}, {}, {}
    for b, grp in pairs:
        bm_t = bms[b, grp].T
        xs_g = xss[b][:, grp * gw:(grp + 1) * gw].astype(BF16)
        a_parts, k_parts, x_parts, col_parts = [], [], [], []
        for hh in range(hpg):
            hd = grp * hpg + hh
            col = jnp.broadcast_to(gs[b][:, d0 + hd:d0 + hd + 1], (c, c))
            dec = jnp.where(tri_mask, jnp.exp(jnp.minimum(col - gt8s[b][hd:hd + 1, :], 0.0)), 0.0)
            a_parts.append((cbs[b, grp] * (dec * dt8s[b][hd:hd + 1, :])).astype(BF16))
            k_parts.append((bm_t * w8s[b][hd:hd + 1, :]).astype(BF16))
            x_parts.append(xs_g * hm_b[hh:hh + 1, :])
            col_parts.append(col)
        a_cat[b, grp] = jnp.concatenate(a_parts, axis=1)
        k_cat[b, grp] = jnp.concatenate(k_parts, axis=1)
        x_bd[b, grp] = jnp.concatenate(x_parts, axis=0)
        e_in[b, grp] = jnp.exp(jnp.concatenate(
            [jnp.where(lane_lo, col_parts[2 * t], col_parts[2 * t + 1]) for t in range(hpg // 2)], axis=1))

    sts = {p: st_ref[p[0], p[1]] for p in pairs}
    intra = {p: _dot(a_cat[p], x_bd[p]) for p in pairs}
    inter = {p: e_in[p] * _dot(cms[p], sts[p].astype(BF16)) for p in pairs}
    for b, grp in pairs:
        st_ref[b, grp] = (sts[b, grp] * gl_xs[b][:, grp * gw:(grp + 1) * gw]
                          + _dot(k_cat[b, grp], x_bd[b, grp]))

    for b in range(nb):
        xs = xss[b]
        o = jnp.concatenate([intra[b, grp] + inter[b, grp] for grp in range(SSD_GROUPS)], axis=1)
        if direction == 0:
            o_ref[b] = o
        else:
            y = (of_ref[b] + o + dsk_ref[...] * xs) * _silu(z_ref[b])
            o_ref[b] = _rms(y, nw_ref[...]).astype(BF16)


def _ssd_call(direction, xconv, small_d, p_z, o_f, dtb_d, alog_d, dsk, nw, consts, *, nctx):
    nb, rb, _ = xconv.shape
    c = SSD_CHUNK
    assert c == LANES
    nch, ncc = rb // c, nctx // c
    kern = functools.partial(_ssd_kernel, direction=direction, nb=nb, c=c)
    chunk = lambda n: (0, _chunk_of(direction, n, ncc, nch), 0)
    const2 = lambda n: (0, 0)
    blk = lambda w: pl.BlockSpec((nb, c, w), chunk)
    full = lambda a: pl.BlockSpec(a.shape, const2)
    ins = [xconv, small_d]
    specs = [blk(SSD_CONV_CH), blk(LANES)]
    if direction == 1:
        ins += [p_z, o_f]
        specs += [blk(SSD_W), blk(SSD_W)]
    tail = [dtb_d, alog_d, consts["tri_ssd"][direction], consts["ssd_x"], consts["hmask"]]
    if direction == 1:
        tail += [dsk, nw]
    ins += tail
    specs += [full(a) for a in tail]
    return pl.pallas_call(
        kern,
        grid=(nch,),
        in_specs=specs,
        out_specs=blk(SSD_W),
        out_shape=jax.ShapeDtypeStruct((nb, rb, SSD_W), F32 if direction == 0 else BF16),
        scratch_shapes=[pltpu.VMEM((nb, SSD_GROUPS, SSD_STATE, 256), F32)],
        compiler_params=_scan_params(),
        name="ssd_fwd" if direction == 0 else "ssd_bwd",
    )(*ins)


def _ret_log_gamma_row(width):
    lane = lax.broadcasted_iota(jnp.int32, (1, width), 1)
    row = jnp.zeros((1, width), F32)
    for h in range(RET_HEADS):
        lg = math.log1p(-(2.0 ** (-5.0 - h)))
        row = jnp.where(lane // RET_HEAD_DIM == h, lg, row)
    return row


def _rope(t, cos, sin_signed):
    lane = lax.broadcasted_iota(jnp.int32, t.shape, 1)
    half = RET_HEAD_DIM // 2
    swapped = jnp.where(lane % RET_HEAD_DIM < half,
                        pltpu.roll(t, LANES - half, 1), pltpu.roll(t, half, 1))
    return t * cos + swapped * sin_signed


def _ret_kernel(*refs, direction, nb, c):
    if direction == 0:
        (p_ref, cos_ref, sin_ref, avg_ref, hm_ref, o_ref, st_ref) = refs
    else:
        (p_ref, cos_ref, sin_ref, of_ref, avg_ref, hm_ref, nw_ref, o_ref, st_ref) = refs
    w = RET_W

    @pl.when(pl.program_id(0) == 0)
    def _():
        st_ref[...] = jnp.zeros_like(st_ref)

    cos = cos_ref[...]
    sin = sin_ref[...]
    lg_row = _ret_log_gamma_row(w)
    pos = lax.broadcasted_iota(jnp.int32, (c, 1), 0)
    cnt_in = (pos + 1 if direction == 0 else c - pos).astype(F32)
    cnt_st = (c - 1 - pos if direction == 0 else pos).astype(F32)
    e_in = jnp.exp(cnt_in * lg_row)
    e_st = jnp.exp(cnt_st * lg_row)
    e_all = jnp.exp(float(c) * lg_row)
    ii = lax.broadcasted_iota(jnp.int32, (c, c), 0)
    jj = lax.broadcasted_iota(jnp.int32, (c, c), 1)
    dist = (ii - jj) if direction == 0 else (jj - ii)
    dec_parts = []
    for h in range(RET_HEADS):
        lg = math.log1p(-(2.0 ** (-5.0 - h)))
        dec_parts.append(jnp.where(dist >= 0, jnp.exp(dist.astype(F32) * lg), 0.0))
    dec = jnp.concatenate(dec_parts, axis=1)
    bd_mask = avg_ref[...].astype(F32)

    hm_b = hm_ref[...].astype(BF16)
    qbs, kbs, vbs, kes = [], [], [], []
    for b in range(nb):
        q_halves, k_halves = [], []
        for hf in range(w // LANES):
            qh = p_ref[b, :, hf * LANES:(hf + 1) * LANES] * (RET_HEAD_DIM ** -0.5)
            kh = p_ref[b, :, w + hf * LANES:w + (hf + 1) * LANES]
            q_halves.append(_rope(qh, cos, sin))
            k_halves.append(_rope(kh, cos, sin))
        k = jnp.concatenate(k_halves, axis=1)
        qbs.append(jnp.concatenate(q_halves, axis=1).astype(BF16))
        kbs.append(k.astype(BF16))
        kes.append((k * e_st).astype(BF16))
        vbs.append(p_ref[b, :, 2 * w:3 * w].astype(BF16))
    k_bd = [jnp.concatenate([kbs[b] * hm_b[h:h + 1, :] for h in range(RET_HEADS)], axis=0) for b in range(nb)]
    v_bd = [jnp.concatenate([vbs[b] * hm_b[h:h + 1, :] for h in range(RET_HEADS)], axis=0) for b in range(nb)]
    scores = [(_dot_nt(qbs[b], k_bd[b]) * dec).astype(BF16) for b in range(nb)]
    intra = [_dot(scores[b], v_bd[b]) for b in range(nb)]
    sts = [st_ref[b] for b in range(nb)]
    outs = [intra[b] + e_in * _dot(qbs[b], sts[b].astype(BF16)) for b in range(nb)]
    for b in range(nb):
        st_ref[b] = sts[b] * e_all + _dot_tn(kes[b], vbs[b]) * bd_mask
    if direction == 0:
        for b in range(nb):
            o_ref[b] = outs[b]
    else:
        tots = [of_ref[b] + outs[b] for b in range(nb)]
        xcs = [tots[b] - _dot_sel_lhs(tots[b], avg_ref[...], 2) * (1.0 / RET_HEAD_DIM) for b in range(nb)]
        vrs = [_dot_sel_lhs(xcs[b] * xcs[b], avg_ref[...], 2) * (1.0 / RET_HEAD_DIM) for b in range(nb)]
        for b in range(nb):
            y = xcs[b] * lax.rsqrt(vrs[b] + RMS_EPS) * nw_ref[...]
            o_ref[b] = (y * _silu(p_ref[b, :, 3 * w:4 * w])).astype(BF16)


def _ret_call(direction, p_ret, cos_t, sin_t, o_f, nw, consts, *, nctx):
    nb, rb, _ = p_ret.shape
    c = RET_CHUNK
    nch, ncc = rb // c, nctx // c
    kern = functools.partial(_ret_kernel, direction=direction, nb=nb, c=c)
    chunk = lambda n: (0, _chunk_of(direction, n, ncc, nch), 0)
    tchunk = lambda n: (_chunk_of(direction, n, ncc, nch), 0)
    const2 = lambda n: (0, 0)
    blk = lambda w: pl.BlockSpec((nb, c, w), chunk)
    full = lambda a: pl.BlockSpec(a.shape, const2)
    ins = [p_ret, cos_t, sin_t]
    specs = [blk(4 * RET_W), pl.BlockSpec((c, LANES), tchunk), pl.BlockSpec((c, LANES), tchunk)]
    if direction == 1:
        ins.append(o_f)
        specs.append(blk(RET_W))
    tail = [consts["avg256"], consts["hmask"]]
    if direction == 1:
        tail.append(nw)
    ins += tail
    specs += [full(a) for a in tail]
    return pl.pallas_call(
        kern,
        grid=(nch,),
        in_specs=specs,
        out_specs=blk(RET_W),
        out_shape=jax.ShapeDtypeStruct((nb, rb, RET_W), F32 if direction == 0 else BF16),
        scratch_shapes=[pltpu.VMEM((nb, RET_W, RET_W), F32)],
        compiler_params=_scan_params(),
        name="ret_fwd" if direction == 0 else "ret_bwd",
    )(*ins)


def _outproj_kernel(h_ref, mod_ref, gla_ref, ssd_ref, ret_ref, wo_ref, nw_ref, o_ref, *, tm, spb, ncs, nb):
    i = pl.program_id(0)
    d = D_MODEL
    mixed = jnp.concatenate([gla_ref[...], ssd_ref[...], ret_ref[...]], axis=1)
    m = _dot(mixed, wo_ref[...])
    for k in range(tm // SLAB):
        mrow = _mod_row(i * (tm // SLAB) + k, spb, ncs, nb)
        gate = mod_ref[mrow][:, 2 * d:3 * d]
        sl = slice(k * SLAB, (k + 1) * SLAB)
        o_ref[sl, :] = h_ref[sl, :] + gate * _rms(m[sl, :], nw_ref[...])


def _outproj_call(h, mod, gla, ssd, ret, wo, nw, *, layer, spb, ncs, nb):
    r, d = h.shape
    tm = TM_DENSE
    kern = functools.partial(_outproj_kernel, tm=tm, spb=spb, ncs=ncs, nb=nb)
    row = lambda i: (i, 0)
    const = lambda i: (0, 0)
    return pl.pallas_call(
        kern,
        grid=(r // tm,),
        in_specs=[
            pl.BlockSpec((tm, d), row),
            pl.BlockSpec(mod.shape, lambda i: (0, 0, 0)),
            pl.BlockSpec((tm, GLA_W), row),
            pl.BlockSpec((tm, SSD_W), row),
            pl.BlockSpec((tm, RET_W), row),
            _layer_spec(wo, layer),
            pl.BlockSpec((1, d), const),
        ],
        out_specs=pl.BlockSpec((tm, d), row),
        out_shape=jax.ShapeDtypeStruct((r, d), F32),
        compiler_params=pltpu.CompilerParams(
            dimension_semantics=("arbitrary",), vmem_limit_bytes=VMEM_LIMIT),
        name="out_proj",
    )(h, mod, gla, ssd, ret, wo, nw)


def _ffn_kernel(h_ref, mod_ref, npre_ref, w13_ref, w2_ref, npost_ref, o_ref, hb_ref, acc_ref,
                *, tm, spb, ncs, nb):
    i = pl.program_id(0)
    d = D_MODEL
    for k in range(tm // SLAB):
        mrow = _mod_row(i * (tm // SLAB) + k, spb, ncs, nb)
        shift = mod_ref[mrow][:, 3 * d:4 * d]
        scale = mod_ref[mrow][:, 4 * d:5 * d]
        sl = slice(k * SLAB, (k + 1) * SLAB)
        y = _rms(h_ref[sl, :], npre_ref[...]) * (1.0 + scale) + shift
        hb_ref[sl, :] = y.astype(BF16)
    hb = hb_ref[...]
    nck = FFN_HIDDEN // FFN_CHUNK

    def gate_up(cidx):
        lo = cidx * FFN_CHUNK
        gate = _dot(hb, w13_ref[:, lo:lo + FFN_CHUNK])
        up = _dot(hb, w13_ref[:, FFN_HIDDEN + lo:FFN_HIDDEN + lo + FFN_CHUNK])
        return (_silu(gate) * up).astype(BF16)

    act = gate_up(0)
    for cidx in range(nck):
        nxt = gate_up(cidx + 1) if cidx + 1 < nck else None
        part = _dot(act, w2_ref[cidx * FFN_CHUNK:(cidx + 1) * FFN_CHUNK, :])
        if cidx == 0:
            acc_ref[...] = part
        else:
            acc_ref[...] += part
        act = nxt
    for k in range(tm // SLAB):
        mrow = _mod_row(i * (tm // SLAB) + k, spb, ncs, nb)
        gate = mod_ref[mrow][:, 5 * d:6 * d]
        sl = slice(k * SLAB, (k + 1) * SLAB)
        o_ref[sl, :] = h_ref[sl, :] + gate * _rms(acc_ref[sl, :], npost_ref[...])


def _ffn_call(h, mod, npre, w13, w2, npost, *, layer, spb, ncs, nb):
    r, d = h.shape
    tm = TM_DENSE
    kern = functools.partial(_ffn_kernel, tm=tm, spb=spb, ncs=ncs, nb=nb)
    row = lambda i: (i, 0)
    const = lambda i: (0, 0)
    return pl.pallas_call(
        kern,
        grid=(r // tm,),
        in_specs=[
            pl.BlockSpec((tm, d), row),
            pl.BlockSpec(mod.shape, lambda i: (0, 0, 0)),
            pl.BlockSpec((1, d), const),
            _layer_spec(w13, layer),
            _layer_spec(w2, layer),
            pl.BlockSpec((1, d), const),
        ],
        out_specs=pl.BlockSpec((tm, d), row),
        out_shape=jax.ShapeDtypeStruct((r, d), F32),
        scratch_shapes=[pltpu.VMEM((tm, d), BF16), pltpu.VMEM((tm, d), F32)],
        compiler_params=pltpu.CompilerParams(
            dimension_semantics=("arbitrary",), vmem_limit_bytes=VMEM_LIMIT),
        name="ffn",
    )(h, mod, npre, w13, w2, npost)


def _tri_pair(c):
    i = jnp.arange(c)[:, None]
    j = jnp.arange(c)[None, :]
    return jnp.stack([(j <= i), (j >= i)]).astype(BF16)


def _gla_cum_pair(cg, sb):
    i = jnp.arange(cg)[:, None]
    j = jnp.arange(cg)[None, :]
    same = (i // sb) == (j // sb)
    fwd = jnp.concatenate([same & (j <= i), same], axis=0)
    bwd = jnp.concatenate([same & (j >= i), same], axis=0)
    return jnp.stack([fwd, bwd]).astype(BF16)


def _constants():
    lane256 = jnp.arange(256)
    consts = {
        "gla_cum": _gla_cum_pair(GLA_BLOCK, GLA_SUB),
        "tri_ssd": _tri_pair(SSD_CHUNK),
        "avg256": (lane256[:, None] // 64 == lane256[None, :] // 64).astype(BF16),
        "gla_e2": (jnp.arange(GLA_KDIM)[:, None] // GLA_HEAD_K == lane256[None, :] // GLA_HEAD_V).astype(BF16),
        "gla_bdm": (lane256[:, None] // GLA_HEAD_V == jnp.arange(GLA_KDIM)[None, :] // GLA_HEAD_K).astype(F32),
        "ssd_x": (jnp.arange(LANES)[:, None] - SMALL_DT_LANE
                  == jnp.arange(SSD_W)[None, :] // SSD_HEAD_DIM).astype(BF16),
        "hmask": (jnp.arange(8)[:, None] == lane256[None, :] // 64).astype(F32),
    }
    return consts


def _rope_tables(t_lat, nctx):
    rows = t_lat // GRID_W
    row = jnp.repeat(jnp.arange(rows), GRID_W).astype(F32)
    col = jnp.tile(jnp.arange(GRID_W), rows).astype(F32)
    n_freq = RET_HEAD_DIM // 4
    inv_freq = ROPE_BASE ** (-jnp.arange(n_freq, dtype=F32) / n_freq)
    ang = jnp.concatenate([row[:, None] * inv_freq, col[:, None] * inv_freq], axis=-1)
    cos = jnp.cos(ang)
    sin = jnp.sin(ang)
    cos_t = jnp.concatenate([cos, cos, cos, cos], axis=-1)
    sin_t = jnp.concatenate([-sin, sin, -sin, sin], axis=-1)
    cos_t = jnp.concatenate([jnp.ones((nctx, LANES), F32), cos_t], axis=0)
    sin_t = jnp.concatenate([jnp.zeros((nctx, LANES), F32), sin_t], axis=0)
    return cos_t, sin_t


def kernel(x, c, ctx, c_ctx, ada_w, ada_b, norm_mix_pre, norm_mix_post, norm_ffn_pre, norm_ffn_post,
           w_in, w_out, gla_gate_up, gla_gate_b, gla_norm, ssd_conv_w, ssd_conv_b, ssd_dt_bias,
           ssd_a_log, ssd_d, ssd_norm, ret_norm, ffn_w13, ffn_w2):
    nb, t_lat, d = x.shape
    nctx = ctx.shape[1]
    depth = ada_w.shape[0]
    rb = nctx + t_lat
    r = nb * rb
    spb = rb // SLAB
    ncs = nctx // SLAB
    assert d == D_MODEL and nb + 1 <= 8
    assert nctx % SLAB == 0 and t_lat % SLAB == 0 and r % TM_DENSE == 0

    consts = _constants()
    cos_t, sin_t = _rope_tables(t_lat, nctx)

    cvec = jnp.zeros((8, d), F32).at[:nb].set(c).at[nb].set(c_ctx)
    mods = _ada_call(cvec, ada_w, ada_b)

    s1, s2 = GLA_COLS, GLA_COLS + SSD_COLS
    lr0 = 2 * GLA_KDIM + 2 * GLA_W
    dt0 = s1 + SSD_W + SSD_CONV_CH
    wm = jnp.concatenate([w_in[:, :, 0:lr0], w_in[:, :, s1:dt0], w_in[:, :, s2:]], axis=-1).astype(BF16)
    zpad = jnp.zeros((depth, d, LANES - GLA_RANK - SSD_HEADS), F32)
    ws = jnp.concatenate(
        [w_in[:, :, lr0:lr0 + GLA_RANK], w_in[:, :, dt0:dt0 + SSD_HEADS], zpad,
         w_in[:, :, lr0 + GLA_RANK:lr0 + 2 * GLA_RANK], w_in[:, :, dt0 + SSD_HEADS:dt0 + 2 * SSD_HEADS], zpad],
        axis=-1).astype(BF16)
    wo = w_out.astype(BF16)
    w13 = ffn_w13.astype(BF16)
    w2 = ffn_w2.astype(BF16)

    wg = jnp.zeros((depth, 2, LANES, GLA_KDIM), F32).at[:, :, :GLA_RANK, :].set(gla_gate_up)
    gb = gla_gate_b.reshape(depth, 2, 1, GLA_KDIM)
    lane_dt = slice(SMALL_DT_LANE, SMALL_DT_LANE + SSD_HEADS)
    dtb = jnp.zeros((depth, 2, 1, LANES), F32).at[:, :, 0, lane_dt].set(ssd_dt_bias)
    alog = jnp.zeros((depth, 2, 1, LANES), F32).at[:, :, 0, lane_dt].set(ssd_a_log)
    dsk = jnp.repeat(ssd_d, SSD_HEAD_DIM, axis=-1).reshape(depth, 1, SSD_W)
    conv_w = jnp.zeros((depth, 8, SSD_CONV_CH), F32).at[:, :SSD_CONV_W].set(ssd_conv_w)

    h = jnp.concatenate([ctx, x], axis=1).reshape(r, d)
    meta = dict(spb=spb, ncs=ncs, nb=nb)
    row1 = lambda a: a.reshape(1, -1)
    for l in range(depth):
        mod = mods[l].reshape(8, 1, 6 * d)
        p_gla, p_z, xconv, p_ret, small_f, small_b = _inproj_call(
            h, mod, row1(norm_mix_pre[l]), wm, ws, conv_w, row1(ssd_conv_b[l]), layer=l, **meta)
        small = (small_f.reshape(nb, rb, LANES), small_b.reshape(nb, rb, LANES))
        xconv = xconv.reshape(nb, rb, SSD_CONV_CH)
        p_gla = p_gla.reshape(nb, rb, 768)
        p_z = p_z.reshape(nb, rb, SSD_W)
        p_ret = p_ret.reshape(nb, rb, 4 * RET_W)
        gla_f = _gla_call(0, p_gla, small[0], None, wg[l, 0], gb[l, 0], None, consts, nctx=nctx)
        gla = _gla_call(1, p_gla, small[1], gla_f, wg[l, 1], gb[l, 1], row1(gla_norm[l]), consts, nctx=nctx)
        ssd_f = _ssd_call(0, xconv, small[0], None, None, dtb[l, 0], alog[l, 0], None, None, consts, nctx=nctx)
        ssd = _ssd_call(1, xconv, small[1], p_z, ssd_f, dtb[l, 1], alog[l, 1], dsk[l], row1(ssd_norm[l]),
                        consts, nctx=nctx)
        ret_f = _ret_call(0, p_ret, cos_t, sin_t, None, None, consts, nctx=nctx)
        ret = _ret_call(1, p_ret, cos_t, sin_t, ret_f, row1(ret_norm[l]), consts, nctx=nctx)
        h = _outproj_call(h, mod, gla.reshape(r, GLA_W), ssd.reshape(r, SSD_W), ret.reshape(r, RET_W),
                          wo, row1(norm_mix_post[l]), layer=l, **meta)
        h = _ffn_call(h, mod, row1(norm_ffn_pre[l]), w13, w2, row1(norm_ffn_post[l]), layer=l, **meta)
    return h.reshape(nb, rb, d)[:, nctx:, :]
```

```python
import functools
import math

import jax
import jax.numpy as jnp
from jax import lax
from jax.experimental import pallas as pl
from jax.experimental.pallas import tpu as pltpu

F32 = jnp.float32
BF16 = jnp.bfloat16

D_MODEL = 1024
GRID_W = 64
RMS_EPS = 1e-6
GLA_W = 256
GLA_HEADS = 4
GLA_HEAD_V = 64
GLA_HEAD_K = 32
GLA_KDIM = 128
GLA_RANK = 16
GLA_GATE_TAU = 16.0
SSD_W = 512
SSD_HEADS = 8
SSD_HEAD_DIM = 64
SSD_GROUPS = 2
SSD_STATE = 128
SSD_CONV_W = 5
SSD_CONV_CH = 1024
RET_W = 256
RET_HEADS = 4
RET_HEAD_DIM = 64
ROPE_BASE = 10000.0
GLA_COLS = 800
SSD_COLS = 1552
FFN_HIDDEN = 2816

LANES = 128
SLAB = 256
TM_DENSE = 512
FFN_CHUNK = 256
GLA_SUB = 16
GLA_BLOCK = 256
SSD_CHUNK = 128
RET_CHUNK = 128
SMALL_DT_LANE = 16
VMEM_LIMIT = 56 * 1024 * 1024


def _silu(x):
    return x * jax.nn.sigmoid(x)


def _dot(a, b):
    return jnp.dot(a, b, preferred_element_type=F32)


def _dot_nt(a, b):
    return lax.dot_general(a, b, (((1,), (1,)), ((), ())), preferred_element_type=F32)


def _dot_tn(a, b):
    return lax.dot_general(a, b, (((0,), (0,)), ((), ())), preferred_element_type=F32)


def _split_bf16(x, terms):
    parts = []
    r = x
    for _ in range(terms):
        hi = r.astype(BF16)
        parts.append(hi)
        r = r - hi.astype(F32)
    return parts


def _dot_sel_rhs(m01, x, terms=3):
    acc = None
    for p in _split_bf16(x, terms):
        t = _dot(m01, p)
        acc = t if acc is None else acc + t
    return acc


def _dot_sel_lhs(x, m01, terms=3):
    acc = None
    for p in _split_bf16(x, terms):
        t = _dot(p, m01)
        acc = t if acc is None else acc + t
    return acc


def _dot3(a, b):
    a_hi = a.astype(BF16)
    a_lo = (a - a_hi.astype(F32)).astype(BF16)
    b_hi = b.astype(BF16)
    b_lo = (b - b_hi.astype(F32)).astype(BF16)
    return _dot(a_hi, b_hi) + _dot(a_lo, b_hi) + _dot(a_hi, b_lo)


def _mod_row(slab, spb, ncs, nb):
    return jnp.where(slab % spb < ncs, nb, slab // spb)


def _rms(x, w):
    return x * lax.rsqrt(jnp.mean(x * x, axis=-1, keepdims=True) + RMS_EPS) * w


def _ada_kernel(cv_ref, w_ref, b_ref, o_ref):
    s = _silu(cv_ref[...]).astype(BF16)
    o_ref[...] = _dot(s, w_ref[...].astype(BF16)) + b_ref[...]


def _ada_call(cvec, ada_w, ada_b):
    depth, d, n6 = ada_w.shape
    tn = 2048
    return pl.pallas_call(
        _ada_kernel,
        grid=(depth, n6 // tn),
        in_specs=[
            pl.BlockSpec((8, d), lambda l, j: (0, 0)),
            pl.BlockSpec((None, d, tn), lambda l, j: (l, 0, j)),
            pl.BlockSpec((None, 1, tn), lambda l, j: (l, 0, j)),
        ],
        out_specs=pl.BlockSpec((None, 8, tn), lambda l, j: (l, 0, j)),
        out_shape=jax.ShapeDtypeStruct((depth, 8, n6), F32),
        compiler_params=pltpu.CompilerParams(
            dimension_semantics=("arbitrary", "arbitrary"), vmem_limit_bytes=VMEM_LIMIT),
        name="ada_mod",
    )(cvec, ada_w, ada_b.reshape(depth, 1, n6))


def _inproj_kernel(h_ref, hp_ref, hn_ref, mod_ref, nw_ref, wm_ref, ws_ref, cw_ref, cb_ref,
                   gla_ref, z_ref, xc_ref, ret_ref, smf_ref, smb_ref, yb_ref, scr_ref, *, tm, spb, ncs, nb):
    i = pl.program_id(0)
    d = D_MODEL
    nsl = tm // SLAB
    x0, x1 = 1280, 2304

    def normed(x, mrow):
        return _rms(x, nw_ref[...]) * (1.0 + mod_ref[mrow][:, d:2 * d]) + mod_ref[mrow][:, 0:d]

    mrows = [_mod_row(i * nsl + k, spb, ncs, nb) for k in range(nsl)]
    for k in range(nsl):
        yb_ref[k * SLAB:(k + 1) * SLAB, :] = normed(h_ref[k * SLAB:(k + 1) * SLAB, :], mrows[k]).astype(BF16)
    yb = yb_ref[...]

    yh = jnp.concatenate([normed(hp_ref[...], mrows[0]), normed(hn_ref[...], mrows[nsl - 1])], axis=0)
    halo = _dot(yh.astype(BF16), wm_ref[:, x0:x1])
    xbc = _dot(yb, wm_ref[:, x0:x1])
    gla_ref[...] = _dot(yb, wm_ref[:, 0:768])
    z_ref[...] = _dot(yb, wm_ref[:, 768:x0])
    ret_ref[...] = _dot(yb, wm_ref[:, x1:3328])
    smf_ref[...] = _dot(yb, ws_ref[:, 0:LANES])
    smb_ref[...] = _dot(yb, ws_ref[:, LANES:2 * LANES])
    pad = (SSD_CONV_W - 1) // 2
    for k in range(nsl):
        pos = (i * nsl + k) % spb
        first = jnp.logical_or(pos == 0, pos == ncs)
        last = jnp.logical_or(pos == ncs - 1, pos == spb - 1)
        before = halo[0:8] if k == 0 else xbc[k * SLAB - 8:k * SLAB, :]
        after = halo[8:16] if k == nsl - 1 else xbc[(k + 1) * SLAB:(k + 1) * SLAB + 8, :]
        scr_ref[0:8, :] = jnp.where(first, 0.0, before)
        scr_ref[8:8 + SLAB, :] = xbc[k * SLAB:(k + 1) * SLAB, :]
        scr_ref[8 + SLAB:16 + SLAB, :] = jnp.where(last, 0.0, after)
        acc = cb_ref[...] + cw_ref[0:1, :] * scr_ref[pl.ds(8 - pad, SLAB), :]
        for t in range(1, SSD_CONV_W):
            acc = acc + cw_ref[t:t + 1, :] * scr_ref[pl.ds(8 - pad + t, SLAB), :]
        xc_ref[k * SLAB:(k + 1) * SLAB, :] = _silu(acc)


def _layer_spec(w, layer):
    zeros = (0,) * (w.ndim - 1)
    return pl.BlockSpec((None,) + w.shape[1:], lambda i: (layer,) + zeros)


def _inproj_call(h, mod, nw, wm, ws, cw, cb, *, layer, spb, ncs, nb):
    r, d = h.shape
    tm = TM_DENSE
    kern = functools.partial(_inproj_kernel, tm=tm, spb=spb, ncs=ncs, nb=nb)
    row = lambda i: (i, 0)
    const = lambda i: (0, 0)
    per = tm // 8
    return pl.pallas_call(
        kern,
        grid=(r // tm,),
        in_specs=[
            pl.BlockSpec((tm, d), row),
            pl.BlockSpec((8, d), lambda i: (jnp.maximum(i * per - 1, 0), 0)),
            pl.BlockSpec((8, d), lambda i: (jnp.minimum((i + 1) * per, r // 8 - 1), 0)),
            pl.BlockSpec(mod.shape, lambda i: (0, 0, 0)),
            pl.BlockSpec((1, d), const),
            _layer_spec(wm, layer),
            _layer_spec(ws, layer),
            _layer_spec(cw, layer),
            pl.BlockSpec(cb.shape, const),
        ],
        out_specs=[
            pl.BlockSpec((tm, 768), row),
            pl.BlockSpec((tm, 512), row),
            pl.BlockSpec((tm, 1024), row),
            pl.BlockSpec((tm, 1024), row),
            pl.BlockSpec((tm, LANES), row),
            pl.BlockSpec((tm, LANES), row),
        ],
        out_shape=[
            jax.ShapeDtypeStruct((r, 768), F32),
            jax.ShapeDtypeStruct((r, 512), F32),
            jax.ShapeDtypeStruct((r, 1024), F32),
            jax.ShapeDtypeStruct((r, 1024), F32),
            jax.ShapeDtypeStruct((r, LANES), F32),
            jax.ShapeDtypeStruct((r, LANES), F32),
        ],
        scratch_shapes=[pltpu.VMEM((tm, d), BF16), pltpu.VMEM((SLAB + 16, SSD_CONV_CH), F32)],
        compiler_params=pltpu.CompilerParams(
            dimension_semantics=("arbitrary",), vmem_limit_bytes=VMEM_LIMIT),
        name="in_proj",
    )(h, h, h, mod, nw, wm, ws, cw, cb)


def _chunk_of(direction, n, ncc, nch):
    if direction == 0:
        return n
    return jnp.where(n < ncc, ncc - 1 - n, nch + ncc - 1 - n)


def _scan_params():
    return pltpu.CompilerParams(dimension_semantics=("arbitrary",), vmem_limit_bytes=VMEM_LIMIT)


def _gla_kernel(*refs, direction, nb, cg):
    if direction == 0:
        (p_ref, sm_ref, wg_ref, gb_ref, cum_ref, e2_ref, bdm_ref, o_ref, st_ref) = refs
    else:
        (p_ref, sm_ref, of_ref, wg_ref, gb_ref, cum_ref, e2_ref, bdm_ref, avg_ref, nw_ref,
         o_ref, st_ref) = refs
    sb = GLA_SUB
    nsb = cg // sb

    @pl.when(pl.program_id(0) == 0)
    def _():
        st_ref[...] = jnp.zeros_like(st_ref)

    half = 8
    rows8 = lax.broadcasted_iota(jnp.int32, (half, GLA_KDIM), 0)
    masks8 = [(rows8 >= j) if direction == 0 else (rows8 <= j) for j in range(half)]

    def half_state(hf, j):
        if hf == j // half:
            return "part"
        kept = hf > j // half if direction == 0 else hf < j // half
        return "full" if kept else "zero"

    pre = []
    for b in range(nb):
        z = _dot3(sm_ref[b], wg_ref[...]) + gb_ref[...]
        logg = (jnp.minimum(z, 0.0) - jnp.log1p(jnp.exp(-jnp.abs(z)))) * (1.0 / GLA_GATE_TAU)
        cums = _dot_sel_rhs(cum_ref[...], logg)
        gs_all = cums[0:cg]
        gl = cums[cg:2 * cg]
        qs_all = p_ref[b, :, 0:GLA_KDIM] * (GLA_HEAD_K ** -0.5)
        k_all = p_ref[b, :, GLA_KDIM:2 * GLA_KDIM]
        v_all = p_ref[b, :, 2 * GLA_KDIM:2 * GLA_KDIM + GLA_W]
        pre.append(dict(
            gs=gs_all, egl=jnp.exp(gl), qs=qs_all, k=k_all, v=v_all,
            qt=(qs_all * jnp.exp(gs_all)).astype(BF16),
            kk=(k_all * jnp.exp(gl - gs_all)).astype(BF16),
            vb=v_all.astype(BF16)))

    sts = [st_ref[b] for b in range(nb)]
    o_parts = [[None] * nsb for _ in range(nb)]
    for it in range(nsb):
        a = it if direction == 0 else nsb - 1 - it
        sl = slice(a * sb, (a + 1) * sb)
        for b in range(nb):
            pb = pre[b]
            gs, qs, kb, vb = pb["gs"][sl], pb["qs"][sl], pb["k"][sl], pb["v"][sl]
            inter = _dot_nt(pb["qt"][sl], sts[b].astype(BF16))
            parts = []
            for j in range(sb):
                halves = []
                for hf in range(sb // half):
                    rs = slice(hf * half, (hf + 1) * half)
                    if half_state(hf, j) == "zero":
                        halves.append(jnp.zeros((half, GLA_KDIM), F32))
                        continue
                    dec = jnp.exp(gs[rs] - gs[j:j + 1, :])
                    if half_state(hf, j) == "part":
                        dec = jnp.where(masks8[j % half], dec, 0.0)
                    halves.append(dec * qs[rs] * kb[j:j + 1, :])
                parts.append(jnp.concatenate(halves, axis=0).astype(BF16))
            sx = _dot(jnp.concatenate(parts, axis=0), e2_ref[...])
            acc = [None] * (sb // half)
            for j in range(sb):
                for hf in range(sb // half):
                    if half_state(hf, j) == "zero":
                        continue
                    term = sx[j * sb + hf * half:j * sb + (hf + 1) * half] * vb[j:j + 1, :]
                    acc[hf] = term if acc[hf] is None else acc[hf] + term
            o_parts[b][a] = jnp.concatenate(acc, axis=0) + inter
            sts[b] = (sts[b] * pb["egl"][a * sb:a * sb + 1, :]
                      + _dot_tn(pb["vb"][sl], pb["kk"][sl]) * bdm_ref[...])
    for b in range(nb):
        st_ref[b] = sts[b]
    outs = [jnp.concatenate(o_parts[b], axis=0) for b in range(nb)]
    if direction == 0:
        for b in range(nb):
            o_ref[b] = outs[b]
    else:
        tots = [of_ref[b] + outs[b] for b in range(nb)]
        mss = [_dot_sel_lhs(tots[b] * tots[b], avg_ref[...], 2) * (1.0 / GLA_HEAD_V) for b in range(nb)]
        for b in range(nb):
            y = tots[b] * lax.rsqrt(mss[b] + RMS_EPS) * nw_ref[...]
            r = p_ref[b, :, 2 * GLA_KDIM + GLA_W:2 * GLA_KDIM + 2 * GLA_W]
            o_ref[b] = (y * _silu(r)).astype(BF16)


def _gla_call(direction, p_gla, small_d, o_f, wg_d, gb_d, nw, consts, *, nctx):
    nb, rb, _ = p_gla.shape
    cg = GLA_BLOCK
    nch, ncc = rb // cg, nctx // cg
    kern = functools.partial(_gla_kernel, direction=direction, nb=nb, cg=cg)
    chunk = lambda n: (0, _chunk_of(direction, n, ncc, nch), 0)
    const2 = lambda n: (0, 0)
    blk = lambda w: pl.BlockSpec((nb, cg, w), chunk)
    full = lambda a: pl.BlockSpec(a.shape, const2)
    cum = consts["gla_cum"][direction]
    ins = [p_gla, small_d]
    specs = [blk(768), blk(LANES)]
    if direction == 1:
        ins.append(o_f)
        specs.append(blk(GLA_W))
    tail = [wg_d, gb_d, cum, consts["gla_e2"], consts["gla_bdm"]]
    if direction == 1:
        tail += [consts["avg256"], nw]
    ins += tail
    specs += [full(a) for a in tail]
    scratch = [pltpu.VMEM((nb, GLA_W, GLA_KDIM), F32)]
    return pl.pallas_call(
        kern,
        grid=(nch,),
        in_specs=specs,
        out_specs=blk(GLA_W),
        out_shape=jax.ShapeDtypeStruct((nb, rb, GLA_W), F32 if direction == 0 else BF16),
        scratch_shapes=scratch,
        compiler_params=_scan_params(),
        name="gla_fwd" if direction == 0 else "gla_bwd",
    )(*ins)


def _ssd_kernel(*refs, direction, nb, c):
    if direction == 0:
        (xc_ref, sm_ref, dtb_ref, alog_ref, tri_ref, x_ref, hm_ref, o_ref, st_ref) = refs
    else:
        (xc_ref, sm_ref, z_ref, of_ref, dtb_ref, alog_ref, tri_ref, x_ref, hm_ref, dsk_ref, nw_ref,
         o_ref, st_ref) = refs
    hpg = SSD_HEADS // SSD_GROUPS
    gw = hpg * SSD_HEAD_DIM

    @pl.when(pl.program_id(0) == 0)
    def _():
        st_ref[...] = jnp.zeros_like(st_ref)

    ii = lax.broadcasted_iota(jnp.int32, (c, c), 0)
    jj = lax.broadcasted_iota(jnp.int32, (c, c), 1)
    tri_mask = (ii >= jj) if direction == 0 else (ii <= jj)
    neg_a = -jnp.exp(alog_ref[...])

    lane_lo = lax.broadcasted_iota(jnp.int32, (c, LANES), 1) < SSD_HEAD_DIM
    hm_b = hm_ref[...].astype(BF16)
    last = c - 1 if direction == 0 else 0
    d0, d1 = SMALL_DT_LANE, SMALL_DT_LANE + SSD_HEADS

    pairs = [(b, grp) for b in range(nb) for grp in range(SSD_GROUPS)]
    dts = []
    for b in range(nb):
        raw = sm_ref[b] + dtb_ref[...]
        dts.append(jnp.maximum(raw, 0.0) + jnp.log1p(jnp.exp(-jnp.abs(raw))))
    gs = [_dot_sel_rhs(tri_ref[...], dts[b] * neg_a) for b in range(nb)]
    gt8s = [gs[b].T[d0:d1, :] for b in range(nb)]
    dt8s = [dts[b].T[d0:d1, :] for b in range(nb)]
    w8s = [dt8s[b] * jnp.exp(gt8s[b][:, last:last + 1] - gt8s[b]) for b in range(nb)]
    gl_xs = [_dot_sel_lhs(jnp.broadcast_to(jnp.exp(gs[b][last:last + 1, :]), (8, LANES)), x_ref[...])[0:1, :]
             for b in range(nb)]
    xss = [xc_ref[b, :, 0:SSD_W] for b in range(nb)]

    cms, bms, cbs = {}, {}, {}
    for b, grp in pairs:
        b0 = SSD_W + grp * SSD_STATE
        c0 = SSD_W + SSD_GROUPS * SSD_STATE + grp * SSD_STATE
        bms[b, grp] = xc_ref[b, :, b0:b0 + SSD_STATE]
        cms[b, grp] = xc_ref[b, :, c0:c0 + SSD_STATE].astype(BF16)
    for b, grp in pairs:
        cbs[b, grp] = _dot_nt(cms[b, grp], bms[b, grp].astype(BF16))

    a_cat, k_cat, x_bd, e_in = {}, {}, {}, {}
    for b, grp in pairs:
        bm_t = bms[b, grp].T
        xs_g = xss[b][:, grp * gw:(grp + 1) * gw].astype(BF16)
        a_parts, k_parts, x_parts, col_parts = [], [], [], []
        for hh in range(hpg):
            hd = grp * hpg + hh
            col = jnp.broadcast_to(gs[b][:, d0 + hd:d0 + hd + 1], (c, c))
            dec = jnp.where(tri_mask, jnp.exp(jnp.minimum(col - gt8s[b][hd:hd + 1, :], 0.0)), 0.0)
            a_parts.append((cbs[b, grp] * (dec * dt8s[b][hd:hd + 1, :])).astype(BF16))
            k_parts.append((bm_t * w8s[b][hd:hd + 1, :]).astype(BF16))
            x_parts.append(xs_g * hm_b[hh:hh + 1, :])
            col_parts.append(col)
        a_cat[b, grp] = jnp.concatenate(a_parts, axis=1)
        k_cat[b, grp] = jnp.concatenate(k_parts, axis=1)
        x_bd[b, grp] = jnp.concatenate(x_parts, axis=0)
        e_in[b, grp] = jnp.exp(jnp.concatenate(
            [jnp.where(lane_lo, col_parts[2 * t], col_parts[2 * t + 1]) for t in range(hpg // 2)], axis=1))

    sts = {p: st_ref[p[0], p[1]] for p in pairs}
    intra = {p: _dot(a_cat[p], x_bd[p]) for p in pairs}
    inter = {p: e_in[p] * _dot(cms[p], sts[p].astype(BF16)) for p in pairs}
    for b, grp in pairs:
        st_ref[b, grp] = (sts[b, grp] * gl_xs[b][:, grp * gw:(grp + 1) * gw]
                          + _dot(k_cat[b, grp], x_bd[b, grp]))

    for b in range(nb):
        xs = xss[b]
        o = jnp.concatenate([intra[b, grp] + inter[b, grp] for grp in range(SSD_GROUPS)], axis=1)
        if direction == 0:
            o_ref[b] = o
        else:
            y = (of_ref[b] + o + dsk_ref[...] * xs) * _silu(z_ref[b])
            o_ref[b] = _rms(y, nw_ref[...]).astype(BF16)


def _ssd_call(direction, xconv, small_d, p_z, o_f, dtb_d, alog_d, dsk, nw, consts, *, nctx):
    nb, rb, _ = xconv.shape
    c = SSD_CHUNK
    assert c == LANES
    nch, ncc = rb // c, nctx // c
    kern = functools.partial(_ssd_kernel, direction=direction, nb=nb, c=c)
    chunk = lambda n: (0, _chunk_of(direction, n, ncc, nch), 0)
    const2 = lambda n: (0, 0)
    blk = lambda w: pl.BlockSpec((nb, c, w), chunk)
    full = lambda a: pl.BlockSpec(a.shape, const2)
    ins = [xconv, small_d]
    specs = [blk(SSD_CONV_CH), blk(LANES)]
    if direction == 1:
        ins += [p_z, o_f]
        specs += [blk(SSD_W), blk(SSD_W)]
    tail = [dtb_d, alog_d, consts["tri_ssd"][direction], consts["ssd_x"], consts["hmask"]]
    if direction == 1:
        tail += [dsk, nw]
    ins += tail
    specs += [full(a) for a in tail]
    return pl.pallas_call(
        kern,
        grid=(nch,),
        in_specs=specs,
        out_specs=blk(SSD_W),
        out_shape=jax.ShapeDtypeStruct((nb, rb, SSD_W), F32 if direction == 0 else BF16),
        scratch_shapes=[pltpu.VMEM((nb, SSD_GROUPS, SSD_STATE, 256), F32)],
        compiler_params=_scan_params(),
        name="ssd_fwd" if direction == 0 else "ssd_bwd",
    )(*ins)


def _ret_log_gamma_row(width):
    lane = lax.broadcasted_iota(jnp.int32, (1, width), 1)
    row = jnp.zeros((1, width), F32)
    for h in range(RET_HEADS):
        lg = math.log1p(-(2.0 ** (-5.0 - h)))
        row = jnp.where(lane // RET_HEAD_DIM == h, lg, row)
    return row


def _rope(t, cos, sin_signed):
    lane = lax.broadcasted_iota(jnp.int32, t.shape, 1)
    half = RET_HEAD_DIM // 2
    swapped = jnp.where(lane % RET_HEAD_DIM < half,
                        pltpu.roll(t, LANES - half, 1), pltpu.roll(t, half, 1))
    return t * cos + swapped * sin_signed


def _ret_kernel(*refs, direction, nb, c):
    if direction == 0:
        (p_ref, cos_ref, sin_ref, avg_ref, hm_ref, o_ref, st_ref) = refs
    else:
        (p_ref, cos_ref, sin_ref, of_ref, avg_ref, hm_ref, nw_ref, o_ref, st_ref) = refs
    w = RET_W

    @pl.when(pl.program_id(0) == 0)
    def _():
        st_ref[...] = jnp.zeros_like(st_ref)

    cos = cos_ref[...]
    sin = sin_ref[...]
    lg_row = _ret_log_gamma_row(w)
    pos = lax.broadcasted_iota(jnp.int32, (c, 1), 0)
    cnt_in = (pos + 1 if direction == 0 else c - pos).astype(F32)
    cnt_st = (c - 1 - pos if direction == 0 else pos).astype(F32)
    e_in = jnp.exp(cnt_in * lg_row)
    e_st = jnp.exp(cnt_st * lg_row)
    e_all = jnp.exp(float(c) * lg_row)
    ii = lax.broadcasted_iota(jnp.int32, (c, c), 0)
    jj = lax.broadcasted_iota(jnp.int32, (c, c), 1)
    dist = (ii - jj) if direction == 0 else (jj - ii)
    dec_parts = []
    for h in range(RET_HEADS):
        lg = math.log1p(-(2.0 ** (-5.0 - h)))
        dec_parts.append(jnp.where(dist >= 0, jnp.exp(dist.astype(F32) * lg), 0.0))
    dec = jnp.concatenate(dec_parts, axis=1)
    bd_mask = avg_ref[...].astype(F32)

    hm_b = hm_ref[...].astype(BF16)
    qbs, kbs, vbs, kes = [], [], [], []
    for b in range(nb):
        q_halves, k_halves = [], []
        for hf in range(w // LANES):
            qh = p_ref[b, :, hf * LANES:(hf + 1) * LANES] * (RET_HEAD_DIM ** -0.5)
            kh = p_ref[b, :, w + hf * LANES:w + (hf + 1) * LANES]
            q_halves.append(_rope(qh, cos, sin))
            k_halves.append(_rope(kh, cos, sin))
        k = jnp.concatenate(k_halves, axis=1)
        qbs.append(jnp.concatenate(q_halves, axis=1).astype(BF16))
        kbs.append(k.astype(BF16))
        kes.append((k * e_st).astype(BF16))
        vbs.append(p_ref[b, :, 2 * w:3 * w].astype(BF16))
    k_bd = [jnp.concatenate([kbs[b] * hm_b[h:h + 1, :] for h in range(RET_HEADS)], axis=0) for b in range(nb)]
    v_bd = [jnp.concatenate([vbs[b] * hm_b[h:h + 1, :] for h in range(RET_HEADS)], axis=0) for b in range(nb)]
    scores = [(_dot_nt(qbs[b], k_bd[b]) * dec).astype(BF16) for b in range(nb)]
    intra = [_dot(scores[b], v_bd[b]) for b in range(nb)]
    sts = [st_ref[b] for b in range(nb)]
    outs = [intra[b] + e_in * _dot(qbs[b], sts[b].astype(BF16)) for b in range(nb)]
    for b in range(nb):
        st_ref[b] = sts[b] * e_all + _dot_tn(kes[b], vbs[b]) * bd_mask
    if direction == 0:
        for b in range(nb):
            o_ref[b] = outs[b]
    else:
        tots = [of_ref[b] + outs[b] for b in range(nb)]
        xcs = [tots[b] - _dot_sel_lhs(tots[b], avg_ref[...], 2) * (1.0 / RET_HEAD_DIM) for b in range(nb)]
        vrs = [_dot_sel_lhs(xcs[b] * xcs[b], avg_ref[...], 2) * (1.0 / RET_HEAD_DIM) for b in range(nb)]
        for b in range(nb):
            y = xcs[b] * lax.rsqrt(vrs[b] + RMS_EPS) * nw_ref[...]
            o_ref[b] = (y * _silu(p_ref[b, :, 3 * w:4 * w])).astype(BF16)


def _ret_call(direction, p_ret, cos_t, sin_t, o_f, nw, consts, *, nctx):
    nb, rb, _ = p_ret.shape
    c = RET_CHUNK
    nch, ncc = rb // c, nctx // c
    kern = functools.partial(_ret_kernel, direction=direction, nb=nb, c=c)
    chunk = lambda n: (0, _chunk_of(direction, n, ncc, nch), 0)
    tchunk = lambda n: (_chunk_of(direction, n, ncc, nch), 0)
    const2 = lambda n: (0, 0)
    blk = lambda w: pl.BlockSpec((nb, c, w), chunk)
    full = lambda a: pl.BlockSpec(a.shape, const2)
    ins = [p_ret, cos_t, sin_t]
    specs = [blk(4 * RET_W), pl.BlockSpec((c, LANES), tchunk), pl.BlockSpec((c, LANES), tchunk)]
    if direction == 1:
        ins.append(o_f)
        specs.append(blk(RET_W))
    tail = [consts["avg256"], consts["hmask"]]
    if direction == 1:
        tail.append(nw)
    ins += tail
    specs += [full(a) for a in tail]
    return pl.pallas_call(
        kern,
        grid=(nch,),
        in_specs=specs,
        out_specs=blk(RET_W),
        out_shape=jax.ShapeDtypeStruct((nb, rb, RET_W), F32 if direction == 0 else BF16),
        scratch_shapes=[pltpu.VMEM((nb, RET_W, RET_W), F32)],
        compiler_params=_scan_params(),
        name="ret_fwd" if direction == 0 else "ret_bwd",
    )(*ins)


def _mixffn_kernel(h0_ref, g0_ref, s0_ref, r0_ref, hn_ref, gn_ref, sn_ref, rn_ref, mod_ref,
                   wo_ref, nmix_ref, npre_ref, w13_ref, w2_ref, npost_ref, o_ref,
                   h1a_ref, h1b_ref, hba_ref, hbb_ref, acc_ref, *, tm, spb, ncs, nb):
    i = pl.program_id(0)
    n = pl.num_programs(0)
    d = D_MODEL
    nsl = tm // SLAB
    nck = FFN_HIDDEN // FFN_CHUNK

    def prologue(h_ref, g_ref, s_ref, r_ref, tile, h1_ref, hb_ref):
        mixed = jnp.concatenate([g_ref[...], s_ref[...], r_ref[...]], axis=1)
        m = _dot(mixed, wo_ref[...])
        for k in range(nsl):
            mod = mod_ref[_mod_row(tile * nsl + k, spb, ncs, nb)]
            sl = slice(k * SLAB, (k + 1) * SLAB)
            h1 = h_ref[sl, :] + mod[:, 2 * d:3 * d] * _rms(m[sl, :], nmix_ref[...])
            h1_ref[sl, :] = h1
            y = _rms(h1, npre_ref[...]) * (1.0 + mod[:, 4 * d:5 * d]) + mod[:, 3 * d:4 * d]
            hb_ref[sl, :] = y.astype(BF16)

    def ffn(h1_ref, hb_ref):
        hb = hb_ref[...]
        for cidx in range(nck):
            lo = cidx * FFN_CHUNK
            gate = _dot(hb, w13_ref[:, lo:lo + FFN_CHUNK])
            up = _dot(hb, w13_ref[:, FFN_HIDDEN + lo:FFN_HIDDEN + lo + FFN_CHUNK])
            part = _dot((_silu(gate) * up).astype(BF16), w2_ref[lo:lo + FFN_CHUNK, :])
            if cidx == 0:
                acc_ref[...] = part
            else:
                acc_ref[...] += part
        for k in range(nsl):
            mod = mod_ref[_mod_row(i * nsl + k, spb, ncs, nb)]
            sl = slice(k * SLAB, (k + 1) * SLAB)
            o_ref[sl, :] = h1_ref[sl, :] + mod[:, 5 * d:6 * d] * _rms(acc_ref[sl, :], npost_ref[...])

    @pl.when(i == 0)
    def _():
        prologue(h0_ref, g0_ref, s0_ref, r0_ref, 0, h1a_ref, hba_ref)

    nxt = jnp.minimum(i + 1, n - 1)

    @pl.when(i % 2 == 0)
    def _():
        prologue(hn_ref, gn_ref, sn_ref, rn_ref, nxt, h1b_ref, hbb_ref)
        ffn(h1a_ref, hba_ref)

    @pl.when(i % 2 == 1)
    def _():
        prologue(hn_ref, gn_ref, sn_ref, rn_ref, nxt, h1a_ref, hba_ref)
        ffn(h1b_ref, hbb_ref)


def _mixffn_call(h, mod, gla, ssd, ret, wo, nmix, npre, w13, w2, npost, *, layer, spb, ncs, nb):
    r, d = h.shape
    tm = TM_DENSE
    nt = r // tm
    kern = functools.partial(_mixffn_kernel, tm=tm, spb=spb, ncs=ncs, nb=nb)
    row = lambda i: (i, 0)
    first = lambda i: (0, 0)
    ahead = lambda i: (jnp.minimum(i + 1, nt - 1), 0)
    const = lambda i: (0, 0)

    def resident(w):
        zeros = (0,) * (w.ndim - 1)
        return pl.BlockSpec((None,) + w.shape[1:], lambda i: (layer,) + zeros, pipeline_mode=pl.Buffered(1))

    tiles = lambda imap: [pl.BlockSpec((tm, d), imap), pl.BlockSpec((tm, GLA_W), imap),
                          pl.BlockSpec((tm, SSD_W), imap), pl.BlockSpec((tm, RET_W), imap)]
    return pl.pallas_call(
        kern,
        grid=(nt,),
        in_specs=tiles(first) + tiles(ahead) + [
            pl.BlockSpec(mod.shape, lambda i: (0, 0, 0)),
            resident(wo),
            pl.BlockSpec((1, d), const),
            pl.BlockSpec((1, d), const),
            resident(w13),
            resident(w2),
            pl.BlockSpec((1, d), const),
        ],
        out_specs=pl.BlockSpec((tm, d), row),
        out_shape=jax.ShapeDtypeStruct((r, d), F32),
        scratch_shapes=[pltpu.VMEM((tm, d), F32), pltpu.VMEM((tm, d), F32),
                        pltpu.VMEM((tm, d), BF16), pltpu.VMEM((tm, d), BF16), pltpu.VMEM((tm, d), F32)],
        compiler_params=pltpu.CompilerParams(
            dimension_semantics=("arbitrary",), vmem_limit_bytes=VMEM_LIMIT),
        name="mix_ffn",
    )(h, gla, ssd, ret, h, gla, ssd, ret, mod, wo, nmix, npre, w13, w2, npost)


def _tri_pair(c):
    i = jnp.arange(c)[:, None]
    j = jnp.arange(c)[None, :]
    return jnp.stack([(j <= i), (j >= i)]).astype(BF16)


def _gla_cum_pair(cg, sb):
    i = jnp.arange(cg)[:, None]
    j = jnp.arange(cg)[None, :]
    same = (i // sb) == (j // sb)
    fwd = jnp.concatenate([same & (j <= i), same], axis=0)
    bwd = jnp.concatenate([same & (j >= i), same], axis=0)
    return jnp.stack([fwd, bwd]).astype(BF16)


def _constants():
    lane256 = jnp.arange(256)
    consts = {
        "gla_cum": _gla_cum_pair(GLA_BLOCK, GLA_SUB),
        "tri_ssd": _tri_pair(SSD_CHUNK),
        "avg256": (lane256[:, None] // 64 == lane256[None, :] // 64).astype(BF16),
        "gla_e2": (jnp.arange(GLA_KDIM)[:, None] // GLA_HEAD_K == lane256[None, :] // GLA_HEAD_V).astype(BF16),
        "gla_bdm": (lane256[:, None] // GLA_HEAD_V == jnp.arange(GLA_KDIM)[None, :] // GLA_HEAD_K).astype(F32),
        "ssd_x": (jnp.arange(LANES)[:, None] - SMALL_DT_LANE
                  == jnp.arange(SSD_W)[None, :] // SSD_HEAD_DIM).astype(BF16),
        "hmask": (jnp.arange(8)[:, None] == lane256[None, :] // 64).astype(F32),
    }
    return consts


def _rope_tables(t_lat, nctx):
    rows = t_lat // GRID_W
    row = jnp.repeat(jnp.arange(rows), GRID_W).astype(F32)
    col = jnp.tile(jnp.arange(GRID_W), rows).astype(F32)
    n_freq = RET_HEAD_DIM // 4
    inv_freq = ROPE_BASE ** (-jnp.arange(n_freq, dtype=F32) / n_freq)
    ang = jnp.concatenate([row[:, None] * inv_freq, col[:, None] * inv_freq], axis=-1)
    cos = jnp.cos(ang)
    sin = jnp.sin(ang)
    cos_t = jnp.concatenate([cos, cos, cos, cos], axis=-1)
    sin_t = jnp.concatenate([-sin, sin, -sin, sin], axis=-1)
    cos_t = jnp.concatenate([jnp.ones((nctx, LANES), F32), cos_t], axis=0)
    sin_t = jnp.concatenate([jnp.zeros((nctx, LANES), F32), sin_t], axis=0)
    return cos_t, sin_t


def kernel(x, c, ctx, c_ctx, ada_w, ada_b, norm_mix_pre, norm_mix_post, norm_ffn_pre, norm_ffn_post,
           w_in, w_out, gla_gate_up, gla_gate_b, gla_norm, ssd_conv_w, ssd_conv_b, ssd_dt_bias,
           ssd_a_log, ssd_d, ssd_norm, ret_norm, ffn_w13, ffn_w2):
    nb, t_lat, d = x.shape
    nctx = ctx.shape[1]
    depth = ada_w.shape[0]
    rb = nctx + t_lat
    r = nb * rb
    spb = rb // SLAB
    ncs = nctx // SLAB
    assert d == D_MODEL and nb + 1 <= 8
    assert nctx % SLAB == 0 and t_lat % SLAB == 0 and r % TM_DENSE == 0

    consts = _constants()
    cos_t, sin_t = _rope_tables(t_lat, nctx)

    cvec = jnp.zeros((8, d), F32).at[:nb].set(c).at[nb].set(c_ctx)
    mods = _ada_call(cvec, ada_w, ada_b)

    s1, s2 = GLA_COLS, GLA_COLS + SSD_COLS
    lr0 = 2 * GLA_KDIM + 2 * GLA_W
    dt0 = s1 + SSD_W + SSD_CONV_CH
    wm = jnp.concatenate([w_in[:, :, 0:lr0], w_in[:, :, s1:dt0], w_in[:, :, s2:]], axis=-1).astype(BF16)
    zpad = jnp.zeros((depth, d, LANES - GLA_RANK - SSD_HEADS), F32)
    ws = jnp.concatenate(
        [w_in[:, :, lr0:lr0 + GLA_RANK], w_in[:, :, dt0:dt0 + SSD_HEADS], zpad,
         w_in[:, :, lr0 + GLA_RANK:lr0 + 2 * GLA_RANK], w_in[:, :, dt0 + SSD_HEADS:dt0 + 2 * SSD_HEADS], zpad],
        axis=-1).astype(BF16)
    wo = w_out.astype(BF16)
    w13 = ffn_w13.astype(BF16)
    w2 = ffn_w2.astype(BF16)

    wg = jnp.zeros((depth, 2, LANES, GLA_KDIM), F32).at[:, :, :GLA_RANK, :].set(gla_gate_up)
    gb = gla_gate_b.reshape(depth, 2, 1, GLA_KDIM)
    lane_dt = slice(SMALL_DT_LANE, SMALL_DT_LANE + SSD_HEADS)
    dtb = jnp.zeros((depth, 2, 1, LANES), F32).at[:, :, 0, lane_dt].set(ssd_dt_bias)
    alog = jnp.zeros((depth, 2, 1, LANES), F32).at[:, :, 0, lane_dt].set(ssd_a_log)
    dsk = jnp.repeat(ssd_d, SSD_HEAD_DIM, axis=-1).reshape(depth, 1, SSD_W)
    conv_w = jnp.zeros((depth, 8, SSD_CONV_CH), F32).at[:, :SSD_CONV_W].set(ssd_conv_w)

    h = jnp.concatenate([ctx, x], axis=1).reshape(r, d)
    meta = dict(spb=spb, ncs=ncs, nb=nb)
    row1 = lambda a: a.reshape(1, -1)
    for l in range(depth):
        mod = mods[l].reshape(8, 1, 6 * d)
        p_gla, p_z, xconv, p_ret, small_f, small_b = _inproj_call(
            h, mod, row1(norm_mix_pre[l]), wm, ws, conv_w, row1(ssd_conv_b[l]), layer=l, **meta)
        small = (small_f.reshape(nb, rb, LANES), small_b.reshape(nb, rb, LANES))
        xconv = xconv.reshape(nb, rb, SSD_CONV_CH)
        p_gla = p_gla.reshape(nb, rb, 768)
        p_z = p_z.reshape(nb, rb, SSD_W)
        p_ret = p_ret.reshape(nb, rb, 4 * RET_W)
        gla_f = _gla_call(0, p_gla, small[0], None, wg[l, 0], gb[l, 0], None, consts, nctx=nctx)
        gla = _gla_call(1, p_gla, small[1], gla_f, wg[l, 1], gb[l, 1], row1(gla_norm[l]), consts, nctx=nctx)
        ssd_f = _ssd_call(0, xconv, small[0], None, None, dtb[l, 0], alog[l, 0], None, None, consts, nctx=nctx)
        ssd = _ssd_call(1, xconv, small[1], p_z, ssd_f, dtb[l, 1], alog[l, 1], dsk[l], row1(ssd_norm[l]),
                        consts, nctx=nctx)
        ret_f = _ret_call(0, p_ret, cos_t, sin_t, None, None, consts, nctx=nctx)
        ret = _ret_call(1, p_ret, cos_t, sin_t, ret_f, row1(ret_norm[l]), consts, nctx=nctx)
        h = _mixffn_call(h, mod, gla.reshape(r, GLA_W), ssd.reshape(r, SSD_W), ret.reshape(r, RET_W),
                         wo, row1(norm_mix_post[l]), row1(norm_ffn_pre[l]), w13, w2, row1(norm_ffn_post[l]),
                         layer=l, **meta)
    return h.reshape(nb, rb, d)[:, nctx:, :]
```

```python
import functools
import math

import jax
import jax.numpy as jnp
from jax import lax
from jax.experimental import pallas as pl
from jax.experimental.pallas import tpu as pltpu

F32 = jnp.float32
BF16 = jnp.bfloat16

D_MODEL = 1024
GRID_W = 64
RMS_EPS = 1e-6
GLA_W = 256
GLA_HEADS = 4
GLA_HEAD_V = 64
GLA_HEAD_K = 32
GLA_KDIM = 128
GLA_RANK = 16
GLA_GATE_TAU = 16.0
SSD_W = 512
SSD_HEADS = 8
SSD_HEAD_DIM = 64
SSD_GROUPS = 2
SSD_STATE = 128
SSD_CONV_W = 5
SSD_CONV_CH = 1024
RET_W = 256
RET_HEADS = 4
RET_HEAD_DIM = 64
ROPE_BASE = 10000.0
GLA_COLS = 800
SSD_COLS = 1552
FFN_HIDDEN = 2816

LANES = 128
SLAB = 256
TM_DENSE = 512
FFN_CHUNK = 256
GLA_SUB = 16
GLA_BLOCK = 256
SSD_CHUNK = 128
RET_CHUNK = 128
SCAN_STEP_ROWS = 256
SMALL_DT_LANE = 16
VMEM_LIMIT = 56 * 1024 * 1024


def _silu(x):
    return x * jax.nn.sigmoid(x)


def _dot(a, b):
    return jnp.dot(a, b, preferred_element_type=F32)


def _dot_nt(a, b):
    return lax.dot_general(a, b, (((1,), (1,)), ((), ())), preferred_element_type=F32)


def _dot_tn(a, b):
    return lax.dot_general(a, b, (((0,), (0,)), ((), ())), preferred_element_type=F32)


def _split_bf16(x, terms):
    parts = []
    r = x
    for _ in range(terms):
        hi = r.astype(BF16)
        parts.append(hi)
        r = r - hi.astype(F32)
    return parts


def _dot_sel_rhs(m01, x, terms=3):
    acc = None
    for p in _split_bf16(x, terms):
        t = _dot(m01, p)
        acc = t if acc is None else acc + t
    return acc


def _dot_sel_lhs(x, m01, terms=3):
    acc = None
    for p in _split_bf16(x, terms):
        t = _dot(p, m01)
        acc = t if acc is None else acc + t
    return acc


def _dot3(a, b):
    a_hi = a.astype(BF16)
    a_lo = (a - a_hi.astype(F32)).astype(BF16)
    b_hi = b.astype(BF16)
    b_lo = (b - b_hi.astype(F32)).astype(BF16)
    return _dot(a_hi, b_hi) + _dot(a_lo, b_hi) + _dot(a_hi, b_lo)


def _mod_row(slab, spb, ncs, nb):
    return jnp.where(slab % spb < ncs, nb, slab // spb)


def _rms(x, w):
    return x * lax.rsqrt(jnp.mean(x * x, axis=-1, keepdims=True) + RMS_EPS) * w


def _ada_kernel(cv_ref, w_ref, b_ref, o_ref):
    s = _silu(cv_ref[...]).astype(BF16)
    o_ref[...] = _dot(s, w_ref[...].astype(BF16)) + b_ref[...]


def _ada_call(cvec, ada_w, ada_b):
    depth, d, n6 = ada_w.shape
    tn = 2048
    return pl.pallas_call(
        _ada_kernel,
        grid=(depth, n6 // tn),
        in_specs=[
            pl.BlockSpec((8, d), lambda l, j: (0, 0)),
            pl.BlockSpec((None, d, tn), lambda l, j: (l, 0, j)),
            pl.BlockSpec((None, 1, tn), lambda l, j: (l, 0, j)),
        ],
        out_specs=pl.BlockSpec((None, 8, tn), lambda l, j: (l, 0, j)),
        out_shape=jax.ShapeDtypeStruct((depth, 8, n6), F32),
        compiler_params=pltpu.CompilerParams(
            dimension_semantics=("arbitrary", "arbitrary"), vmem_limit_bytes=VMEM_LIMIT),
        name="ada_mod",
    )(cvec, ada_w, ada_b.reshape(depth, 1, n6))


def _inproj_kernel(h_ref, hp_ref, hn_ref, mod_ref, nw_ref, wm_ref, ws_ref, cw_ref, cb_ref,
                   gla_ref, z_ref, xc_ref, ret_ref, smf_ref, smb_ref, yb_ref, scr_ref, *, tm, spb, ncs, nb):
    i = pl.program_id(0)
    d = D_MODEL
    nsl = tm // SLAB
    x0, x1 = 1280, 2304

    def normed(x, mrow):
        return _rms(x, nw_ref[...]) * (1.0 + mod_ref[mrow][:, d:2 * d]) + mod_ref[mrow][:, 0:d]

    mrows = [_mod_row(i * nsl + k, spb, ncs, nb) for k in range(nsl)]
    for k in range(nsl):
        yb_ref[k * SLAB:(k + 1) * SLAB, :] = normed(h_ref[k * SLAB:(k + 1) * SLAB, :], mrows[k]).astype(BF16)
    yb = yb_ref[...]

    yh = jnp.concatenate([normed(hp_ref[...], mrows[0]), normed(hn_ref[...], mrows[nsl - 1])], axis=0)
    halo = _dot(yh.astype(BF16), wm_ref[:, x0:x1])
    xbc = _dot(yb, wm_ref[:, x0:x1])
    gla_ref[...] = _dot(yb, wm_ref[:, 0:768])
    z_ref[...] = _dot(yb, wm_ref[:, 768:x0])
    ret_ref[...] = _dot(yb, wm_ref[:, x1:3328])
    smf_ref[...] = _dot(yb, ws_ref[:, 0:LANES])
    smb_ref[...] = _dot(yb, ws_ref[:, LANES:2 * LANES])
    pad = (SSD_CONV_W - 1) // 2
    for k in range(nsl):
        pos = (i * nsl + k) % spb
        first = jnp.logical_or(pos == 0, pos == ncs)
        last = jnp.logical_or(pos == ncs - 1, pos == spb - 1)
        before = halo[0:8] if k == 0 else xbc[k * SLAB - 8:k * SLAB, :]
        after = halo[8:16] if k == nsl - 1 else xbc[(k + 1) * SLAB:(k + 1) * SLAB + 8, :]
        scr_ref[0:8, :] = jnp.where(first, 0.0, before)
        scr_ref[8:8 + SLAB, :] = xbc[k * SLAB:(k + 1) * SLAB, :]
        scr_ref[8 + SLAB:16 + SLAB, :] = jnp.where(last, 0.0, after)
        ext = scr_ref[...]
        acc = cb_ref[...] + cw_ref[pad:pad + 1, :] * ext[8:8 + SLAB]
        for t in range(SSD_CONV_W):
            if t != pad:
                shifted = pltpu.roll(ext, (pad - t) % (SLAB + 16), 0)[8:8 + SLAB]
                acc = acc + cw_ref[t:t + 1, :] * shifted
        xc_ref[k * SLAB:(k + 1) * SLAB, :] = _silu(acc)


def _layer_spec(w, layer):
    zeros = (0,) * (w.ndim - 1)
    return pl.BlockSpec((None,) + w.shape[1:], lambda i: (layer,) + zeros)


def _inproj_call(h, mod, nw, wm, ws, cw, cb, *, layer, spb, ncs, nb):
    r, d = h.shape
    tm = TM_DENSE
    kern = functools.partial(_inproj_kernel, tm=tm, spb=spb, ncs=ncs, nb=nb)
    row = lambda i: (i, 0)
    const = lambda i: (0, 0)
    per = tm // 8
    return pl.pallas_call(
        kern,
        grid=(r // tm,),
        in_specs=[
            pl.BlockSpec((tm, d), row),
            pl.BlockSpec((8, d), lambda i: (jnp.maximum(i * per - 1, 0), 0)),
            pl.BlockSpec((8, d), lambda i: (jnp.minimum((i + 1) * per, r // 8 - 1), 0)),
            pl.BlockSpec(mod.shape, lambda i: (0, 0, 0)),
            pl.BlockSpec((1, d), const),
            _layer_spec(wm, layer),
            _layer_spec(ws, layer),
            _layer_spec(cw, layer),
            pl.BlockSpec(cb.shape, const),
        ],
        out_specs=[
            pl.BlockSpec((tm, 768), row),
            pl.BlockSpec((tm, 512), row),
            pl.BlockSpec((tm, 1024), row),
            pl.BlockSpec((tm, 1024), row),
            pl.BlockSpec((tm, LANES), row),
            pl.BlockSpec((tm, LANES), row),
        ],
        out_shape=[
            jax.ShapeDtypeStruct((r, 768), F32),
            jax.ShapeDtypeStruct((r, 512), F32),
            jax.ShapeDtypeStruct((r, 1024), F32),
            jax.ShapeDtypeStruct((r, 1024), F32),
            jax.ShapeDtypeStruct((r, LANES), F32),
            jax.ShapeDtypeStruct((r, LANES), F32),
        ],
        scratch_shapes=[pltpu.VMEM((tm, d), BF16), pltpu.VMEM((SLAB + 16, SSD_CONV_CH), F32)],
        compiler_params=pltpu.CompilerParams(
            dimension_semantics=("arbitrary",), vmem_limit_bytes=VMEM_LIMIT),
        name="in_proj",
    )(h, h, h, mod, nw, wm, ws, cw, cb)


def _chunk_of(direction, n, ncc, nch):
    if direction == 0:
        return n
    return jnp.where(n < ncc, ncc - 1 - n, nch + ncc - 1 - n)


def _scan_params():
    return pltpu.CompilerParams(dimension_semantics=("arbitrary",), vmem_limit_bytes=VMEM_LIMIT)


def _gla_kernel(*refs, direction, nb, cg):
    if direction == 0:
        (p_ref, sm_ref, wg_ref, gb_ref, cum_ref, e2_ref, bdm_ref, o_ref, st_ref) = refs
    else:
        (p_ref, sm_ref, of_ref, wg_ref, gb_ref, cum_ref, e2_ref, bdm_ref, avg_ref, nw_ref,
         o_ref, st_ref) = refs
    sb = GLA_SUB
    nsb = cg // sb

    @pl.when(pl.program_id(0) == 0)
    def _():
        st_ref[...] = jnp.zeros_like(st_ref)

    half = 8
    rows8 = lax.broadcasted_iota(jnp.int32, (half, GLA_KDIM), 0)
    masks8 = [(rows8 >= j) if direction == 0 else (rows8 <= j) for j in range(half)]

    def half_state(hf, j):
        if hf == j // half:
            return "part"
        kept = hf > j // half if direction == 0 else hf < j // half
        return "full" if kept else "zero"

    pre = []
    for b in range(nb):
        z = _dot3(sm_ref[b], wg_ref[...]) + gb_ref[...]
        logg = (jnp.minimum(z, 0.0) - jnp.log1p(jnp.exp(-jnp.abs(z)))) * (1.0 / GLA_GATE_TAU)
        cums = _dot_sel_rhs(cum_ref[...], logg)
        gs_all = cums[0:cg]
        gl = cums[cg:2 * cg]
        qs_all = p_ref[b, :, 0:GLA_KDIM] * (GLA_HEAD_K ** -0.5)
        k_all = p_ref[b, :, GLA_KDIM:2 * GLA_KDIM]
        v_all = p_ref[b, :, 2 * GLA_KDIM:2 * GLA_KDIM + GLA_W]
        pre.append(dict(
            gs=gs_all, egl=jnp.exp(gl), qs=qs_all, k=k_all, v=v_all,
            qt=(qs_all * jnp.exp(gs_all)).astype(BF16),
            kk=(k_all * jnp.exp(gl - gs_all)).astype(BF16),
            vb=v_all.astype(BF16)))

    sts = [st_ref[b] for b in range(nb)]
    o_parts = [[None] * nsb for _ in range(nb)]
    for it in range(nsb):
        a = it if direction == 0 else nsb - 1 - it
        sl = slice(a * sb, (a + 1) * sb)
        for b in range(nb):
            pb = pre[b]
            gs, qs, kb, vb = pb["gs"][sl], pb["qs"][sl], pb["k"][sl], pb["v"][sl]
            inter = _dot_nt(pb["qt"][sl], sts[b].astype(BF16))
            parts = []
            for j in range(sb):
                halves = []
                for hf in range(sb // half):
                    rs = slice(hf * half, (hf + 1) * half)
                    if half_state(hf, j) == "zero":
                        halves.append(jnp.zeros((half, GLA_KDIM), F32))
                        continue
                    dec = jnp.exp(gs[rs] - gs[j:j + 1, :])
                    if half_state(hf, j) == "part":
                        dec = jnp.where(masks8[j % half], dec, 0.0)
                    halves.append(dec * qs[rs] * kb[j:j + 1, :])
                parts.append(jnp.concatenate(halves, axis=0).astype(BF16))
            sx = _dot(jnp.concatenate(parts, axis=0), e2_ref[...])
            acc = [None] * (sb // half)
            for j in range(sb):
                for hf in range(sb // half):
                    if half_state(hf, j) == "zero":
                        continue
                    term = sx[j * sb + hf * half:j * sb + (hf + 1) * half] * vb[j:j + 1, :]
                    acc[hf] = term if acc[hf] is None else acc[hf] + term
            o_parts[b][a] = jnp.concatenate(acc, axis=0) + inter
            sts[b] = (sts[b] * pb["egl"][a * sb:a * sb + 1, :]
                      + _dot_tn(pb["vb"][sl], pb["kk"][sl]) * bdm_ref[...])
    for b in range(nb):
        st_ref[b] = sts[b]
    outs = [jnp.concatenate(o_parts[b], axis=0) for b in range(nb)]
    if direction == 0:
        for b in range(nb):
            o_ref[b] = outs[b]
    else:
        tots = [of_ref[b] + outs[b] for b in range(nb)]
        mss = [_dot_sel_lhs(tots[b] * tots[b], avg_ref[...], 2) * (1.0 / GLA_HEAD_V) for b in range(nb)]
        for b in range(nb):
            y = tots[b] * lax.rsqrt(mss[b] + RMS_EPS) * nw_ref[...]
            r = p_ref[b, :, 2 * GLA_KDIM + GLA_W:2 * GLA_KDIM + 2 * GLA_W]
            o_ref[b] = (y * _silu(r)).astype(BF16)


def _gla_call(direction, p_gla, small_d, o_f, wg_d, gb_d, nw, consts, *, nctx):
    nb, rb, _ = p_gla.shape
    cg = GLA_BLOCK
    nch, ncc = rb // cg, nctx // cg
    kern = functools.partial(_gla_kernel, direction=direction, nb=nb, cg=cg)
    chunk = lambda n: (0, _chunk_of(direction, n, ncc, nch), 0)
    const2 = lambda n: (0, 0)
    blk = lambda w: pl.BlockSpec((nb, cg, w), chunk)
    full = lambda a: pl.BlockSpec(a.shape, const2)
    cum = consts["gla_cum"][direction]
    ins = [p_gla, small_d]
    specs = [blk(768), blk(LANES)]
    if direction == 1:
        ins.append(o_f)
        specs.append(blk(GLA_W))
    tail = [wg_d, gb_d, cum, consts["gla_e2"], consts["gla_bdm"]]
    if direction == 1:
        tail += [consts["avg256"], nw]
    ins += tail
    specs += [full(a) for a in tail]
    scratch = [pltpu.VMEM((nb, GLA_W, GLA_KDIM), F32)]
    return pl.pallas_call(
        kern,
        grid=(nch,),
        in_specs=specs,
        out_specs=blk(GLA_W),
        out_shape=jax.ShapeDtypeStruct((nb, rb, GLA_W), F32 if direction == 0 else BF16),
        scratch_shapes=scratch,
        compiler_params=_scan_params(),
        name="gla_fwd" if direction == 0 else "gla_bwd",
    )(*ins)


def _ssd_kernel(*refs, direction, nb, c, nck):
    if direction == 0:
        (xc_ref, sm_ref, dtb_ref, alog_ref, tri_ref, x_ref, hm_ref, o_ref, st_ref) = refs
    else:
        (xc_ref, sm_ref, z_ref, of_ref, dtb_ref, alog_ref, tri_ref, x_ref, hm_ref, dsk_ref, nw_ref,
         o_ref, st_ref) = refs
    hpg = SSD_HEADS // SSD_GROUPS
    gw = hpg * SSD_HEAD_DIM

    @pl.when(pl.program_id(0) == 0)
    def _():
        st_ref[...] = jnp.zeros_like(st_ref)

    ii = lax.broadcasted_iota(jnp.int32, (c, c), 0)
    jj = lax.broadcasted_iota(jnp.int32, (c, c), 1)
    tri_mask = (ii >= jj) if direction == 0 else (ii <= jj)
    neg_a = -jnp.exp(alog_ref[...])

    lane_lo = lax.broadcasted_iota(jnp.int32, (c, LANES), 1) < SSD_HEAD_DIM
    hm_b = hm_ref[...].astype(BF16)
    last = c - 1 if direction == 0 else 0
    d0, d1 = SMALL_DT_LANE, SMALL_DT_LANE + SSD_HEADS

    order = list(range(nck)) if direction == 0 else list(range(nck - 1, -1, -1))
    rows = {ci: slice(ci * c, (ci + 1) * c) for ci in order}
    units = [(ci, b) for ci in order for b in range(nb)]
    upairs = [(ci, b, grp) for ci, b in units for grp in range(SSD_GROUPS)]
    dts = {}
    for ci, b in units:
        raw = sm_ref[b, rows[ci], :] + dtb_ref[...]
        dts[ci, b] = jnp.maximum(raw, 0.0) + jnp.log1p(jnp.exp(-jnp.abs(raw)))
    gs = {u: _dot_sel_rhs(tri_ref[...], dts[u] * neg_a) for u in units}
    gt8s = {u: gs[u].T[d0:d1, :] for u in units}
    dt8s = {u: dts[u].T[d0:d1, :] for u in units}
    w8s = {u: dt8s[u] * jnp.exp(gt8s[u][:, last:last + 1] - gt8s[u]) for u in units}
    gl_xs = {u: _dot_sel_lhs(jnp.broadcast_to(jnp.exp(gs[u][last:last + 1, :]), (8, LANES)), x_ref[...])[0:1, :]
             for u in units}
    xss = {(ci, b): xc_ref[b, rows[ci], 0:SSD_W] for ci, b in units}

    cms, bms, cbs = {}, {}, {}
    for ci, b, grp in upairs:
        b0 = SSD_W + grp * SSD_STATE
        c0 = SSD_W + SSD_GROUPS * SSD_STATE + grp * SSD_STATE
        bms[ci, b, grp] = xc_ref[b, rows[ci], b0:b0 + SSD_STATE]
        cms[ci, b, grp] = xc_ref[b, rows[ci], c0:c0 + SSD_STATE].astype(BF16)
    for p in upairs:
        cbs[p] = _dot_nt(cms[p], bms[p].astype(BF16))

    a_cat, k_cat, x_bd, e_in = {}, {}, {}, {}
    for ci, b, grp in upairs:
        u = (ci, b)
        bm_t = bms[ci, b, grp].T
        xs_g = xss[u][:, grp * gw:(grp + 1) * gw].astype(BF16)
        a_parts, k_parts, x_parts, col_parts = [], [], [], []
        for hh in range(hpg):
            hd = grp * hpg + hh
            col = jnp.broadcast_to(gs[u][:, d0 + hd:d0 + hd + 1], (c, c))
            dec = jnp.where(tri_mask, jnp.exp(jnp.minimum(col - gt8s[u][hd:hd + 1, :], 0.0)), 0.0)
            a_parts.append((cbs[ci, b, grp] * (dec * dt8s[u][hd:hd + 1, :])).astype(BF16))
            k_parts.append((bm_t * w8s[u][hd:hd + 1, :]).astype(BF16))
            x_parts.append(xs_g * hm_b[hh:hh + 1, :])
            col_parts.append(col)
        a_cat[ci, b, grp] = jnp.concatenate(a_parts, axis=1)
        k_cat[ci, b, grp] = jnp.concatenate(k_parts, axis=1)
        x_bd[ci, b, grp] = jnp.concatenate(x_parts, axis=0)
        e_in[ci, b, grp] = jnp.exp(jnp.concatenate(
            [jnp.where(lane_lo, col_parts[2 * t], col_parts[2 * t + 1]) for t in range(hpg // 2)], axis=1))

    intra = {p: _dot(a_cat[p], x_bd[p]) for p in upairs}
    sts = {(b, grp): st_ref[b, grp] for b in range(nb) for grp in range(SSD_GROUPS)}
    inter = {}
    for ci in order:
        for b in range(nb):
            for grp in range(SSD_GROUPS):
                inter[ci, b, grp] = e_in[ci, b, grp] * _dot(cms[ci, b, grp], sts[b, grp].astype(BF16))
        for b in range(nb):
            for grp in range(SSD_GROUPS):
                sts[b, grp] = (sts[b, grp] * gl_xs[ci, b][:, grp * gw:(grp + 1) * gw]
                               + _dot(k_cat[ci, b, grp], x_bd[ci, b, grp]))
    for b in range(nb):
        for grp in range(SSD_GROUPS):
            st_ref[b, grp] = sts[b, grp]

    for ci, b in units:
        o = jnp.concatenate([intra[ci, b, grp] + inter[ci, b, grp] for grp in range(SSD_GROUPS)], axis=1)
        if direction == 0:
            o_ref[b, rows[ci], :] = o
        else:
            y = (of_ref[b, rows[ci], :] + o + dsk_ref[...] * xss[ci, b]) * _silu(z_ref[b, rows[ci], :])
            o_ref[b, rows[ci], :] = _rms(y, nw_ref[...]).astype(BF16)


def _ssd_call(direction, xconv, small_d, p_z, o_f, dtb_d, alog_d, dsk, nw, consts, *, nctx):
    nb, rb, _ = xconv.shape
    c = SSD_CHUNK
    assert c == LANES
    nck = SCAN_STEP_ROWS // c
    nch, ncc = rb // SCAN_STEP_ROWS, nctx // SCAN_STEP_ROWS
    kern = functools.partial(_ssd_kernel, direction=direction, nb=nb, c=c, nck=nck)
    chunk = lambda n: (0, _chunk_of(direction, n, ncc, nch), 0)
    const2 = lambda n: (0, 0)
    blk = lambda w: pl.BlockSpec((nb, SCAN_STEP_ROWS, w), chunk)
    full = lambda a: pl.BlockSpec(a.shape, const2)
    ins = [xconv, small_d]
    specs = [blk(SSD_CONV_CH), blk(LANES)]
    if direction == 1:
        ins += [p_z, o_f]
        specs += [blk(SSD_W), blk(SSD_W)]
    tail = [dtb_d, alog_d, consts["tri_ssd"][direction], consts["ssd_x"], consts["hmask"]]
    if direction == 1:
        tail += [dsk, nw]
    ins += tail
    specs += [full(a) for a in tail]
    return pl.pallas_call(
        kern,
        grid=(nch,),
        in_specs=specs,
        out_specs=blk(SSD_W),
        out_shape=jax.ShapeDtypeStruct((nb, rb, SSD_W), F32 if direction == 0 else BF16),
        scratch_shapes=[pltpu.VMEM((nb, SSD_GROUPS, SSD_STATE, 256), F32)],
        compiler_params=_scan_params(),
        name="ssd_fwd" if direction == 0 else "ssd_bwd",
    )(*ins)


def _ret_log_gamma_row(width):
    lane = lax.broadcasted_iota(jnp.int32, (1, width), 1)
    row = jnp.zeros((1, width), F32)
    for h in range(RET_HEADS):
        lg = math.log1p(-(2.0 ** (-5.0 - h)))
        row = jnp.where(lane // RET_HEAD_DIM == h, lg, row)
    return row


def _rope(t, cos, sin_signed):
    lane = lax.broadcasted_iota(jnp.int32, t.shape, 1)
    half = RET_HEAD_DIM // 2
    swapped = jnp.where(lane % RET_HEAD_DIM < half,
                        pltpu.roll(t, LANES - half, 1), pltpu.roll(t, half, 1))
    return t * cos + swapped * sin_signed


def _ret_kernel(*refs, direction, nb, c, nck):
    if direction == 0:
        (p_ref, cos_ref, sin_ref, avg_ref, hm_ref, o_ref, st_ref) = refs
    else:
        (p_ref, cos_ref, sin_ref, of_ref, avg_ref, hm_ref, nw_ref, o_ref, st_ref) = refs
    w = RET_W

    @pl.when(pl.program_id(0) == 0)
    def _():
        st_ref[...] = jnp.zeros_like(st_ref)

    lg_row = _ret_log_gamma_row(w)
    pos = lax.broadcasted_iota(jnp.int32, (c, 1), 0)
    cnt_in = (pos + 1 if direction == 0 else c - pos).astype(F32)
    cnt_st = (c - 1 - pos if direction == 0 else pos).astype(F32)
    e_in = jnp.exp(cnt_in * lg_row)
    e_st = jnp.exp(cnt_st * lg_row)
    e_all = jnp.exp(float(c) * lg_row)
    ii = lax.broadcasted_iota(jnp.int32, (c, c), 0)
    jj = lax.broadcasted_iota(jnp.int32, (c, c), 1)
    dist = (ii - jj) if direction == 0 else (jj - ii)
    dec_parts = []
    for h in range(RET_HEADS):
        lg = math.log1p(-(2.0 ** (-5.0 - h)))
        dec_parts.append(jnp.where(dist >= 0, jnp.exp(dist.astype(F32) * lg), 0.0))
    dec = jnp.concatenate(dec_parts, axis=1)
    bd_mask = avg_ref[...].astype(F32)

    order = list(range(nck)) if direction == 0 else list(range(nck - 1, -1, -1))
    rows = {ci: slice(ci * c, (ci + 1) * c) for ci in order}
    units = [(ci, b) for ci in order for b in range(nb)]
    hm_b = hm_ref[...].astype(BF16)
    qbs, kbs, vbs, kes = {}, {}, {}, {}
    for ci, b in units:
        cos = cos_ref[rows[ci], :]
        sin = sin_ref[rows[ci], :]
        q_halves, k_halves = [], []
        for hf in range(w // LANES):
            qh = p_ref[b, rows[ci], hf * LANES:(hf + 1) * LANES] * (RET_HEAD_DIM ** -0.5)
            kh = p_ref[b, rows[ci], w + hf * LANES:w + (hf + 1) * LANES]
            q_halves.append(_rope(qh, cos, sin))
            k_halves.append(_rope(kh, cos, sin))
        k = jnp.concatenate(k_halves, axis=1)
        qbs[ci, b] = jnp.concatenate(q_halves, axis=1).astype(BF16)
        kbs[ci, b] = k.astype(BF16)
        kes[ci, b] = (k * e_st).astype(BF16)
        vbs[ci, b] = p_ref[b, rows[ci], 2 * w:3 * w].astype(BF16)
    k_bd = {u: jnp.concatenate([kbs[u] * hm_b[h:h + 1, :] for h in range(RET_HEADS)], axis=0) for u in units}
    v_bd = {u: jnp.concatenate([vbs[u] * hm_b[h:h + 1, :] for h in range(RET_HEADS)], axis=0) for u in units}
    scores = {u: (_dot_nt(qbs[u], k_bd[u]) * dec).astype(BF16) for u in units}
    intra = {u: _dot(scores[u], v_bd[u]) for u in units}
    sts = [st_ref[b] for b in range(nb)]
    outs = {}
    for ci in order:
        for b in range(nb):
            outs[ci, b] = intra[ci, b] + e_in * _dot(qbs[ci, b], sts[b].astype(BF16))
        for b in range(nb):
            sts[b] = sts[b] * e_all + _dot_tn(kes[ci, b], vbs[ci, b]) * bd_mask
    for b in range(nb):
        st_ref[b] = sts[b]
    if direction == 0:
        for ci, b in units:
            o_ref[b, rows[ci], :] = outs[ci, b]
    else:
        tots = {(ci, b): of_ref[b, rows[ci], :] + outs[ci, b] for ci, b in units}
        xcs = {u: tots[u] - _dot_sel_lhs(tots[u], avg_ref[...], 2) * (1.0 / RET_HEAD_DIM) for u in units}
        vrs = {u: _dot_sel_lhs(xcs[u] * xcs[u], avg_ref[...], 2) * (1.0 / RET_HEAD_DIM) for u in units}
        for ci, b in units:
            y = xcs[ci, b] * lax.rsqrt(vrs[ci, b] + RMS_EPS) * nw_ref[...]
            o_ref[b, rows[ci], :] = (y * _silu(p_ref[b, rows[ci], 3 * w:4 * w])).astype(BF16)


def _ret_call(direction, p_ret, cos_t, sin_t, o_f, nw, consts, *, nctx):
    nb, rb, _ = p_ret.shape
    c = RET_CHUNK
    step = SCAN_STEP_ROWS
    nch, ncc = rb // step, nctx // step
    kern = functools.partial(_ret_kernel, direction=direction, nb=nb, c=c, nck=step // c)
    chunk = lambda n: (0, _chunk_of(direction, n, ncc, nch), 0)
    tchunk = lambda n: (_chunk_of(direction, n, ncc, nch), 0)
    const2 = lambda n: (0, 0)
    blk = lambda w: pl.BlockSpec((nb, step, w), chunk)
    full = lambda a: pl.BlockSpec(a.shape, const2)
    ins = [p_ret, cos_t, sin_t]
    specs = [blk(4 * RET_W), pl.BlockSpec((step, LANES), tchunk), pl.BlockSpec((step, LANES), tchunk)]
    if direction == 1:
        ins.append(o_f)
        specs.append(blk(RET_W))
    tail = [consts["avg256"], consts["hmask"]]
    if direction == 1:
        tail.append(nw)
    ins += tail
    specs += [full(a) for a in tail]
    return pl.pallas_call(
        kern,
        grid=(nch,),
        in_specs=specs,
        out_specs=blk(RET_W),
        out_shape=jax.ShapeDtypeStruct((nb, rb, RET_W), F32 if direction == 0 else BF16),
        scratch_shapes=[pltpu.VMEM((nb, RET_W, RET_W), F32)],
        compiler_params=_scan_params(),
        name="ret_fwd" if direction == 0 else "ret_bwd",
    )(*ins)


def _mixffn_kernel(h0_ref, g0_ref, s0_ref, r0_ref, hn_ref, gn_ref, sn_ref, rn_ref, mod_ref,
                   wo_ref, nmix_ref, npre_ref, w13_ref, w2_ref, npost_ref, o_ref,
                   h1a_ref, h1b_ref, hba_ref, hbb_ref, acc_ref, *, tm, spb, ncs, nb):
    i = pl.program_id(0)
    n = pl.num_programs(0)
    d = D_MODEL
    nsl = tm // SLAB
    nck = FFN_HIDDEN // FFN_CHUNK

    def prologue(h_ref, g_ref, s_ref, r_ref, tile, h1_ref, hb_ref):
        mixed = jnp.concatenate([g_ref[...], s_ref[...], r_ref[...]], axis=1)
        m = _dot(mixed, wo_ref[...])
        for k in range(nsl):
            mod = mod_ref[_mod_row(tile * nsl + k, spb, ncs, nb)]
            sl = slice(k * SLAB, (k + 1) * SLAB)
            h1 = h_ref[sl, :] + mod[:, 2 * d:3 * d] * _rms(m[sl, :], nmix_ref[...])
            h1_ref[sl, :] = h1
            y = _rms(h1, npre_ref[...]) * (1.0 + mod[:, 4 * d:5 * d]) + mod[:, 3 * d:4 * d]
            hb_ref[sl, :] = y.astype(BF16)

    def ffn(h1_ref, hb_ref):
        hb = hb_ref[...]
        for cidx in range(nck):
            lo = cidx * FFN_CHUNK
            gate = _dot(hb, w13_ref[:, lo:lo + FFN_CHUNK])
            up = _dot(hb, w13_ref[:, FFN_HIDDEN + lo:FFN_HIDDEN + lo + FFN_CHUNK])
            part = _dot((_silu(gate) * up).astype(BF16), w2_ref[lo:lo + FFN_CHUNK, :])
            if cidx == 0:
                acc_ref[...] = part
            else:
                acc_ref[...] += part
        for k in range(nsl):
            mod = mod_ref[_mod_row(i * nsl + k, spb, ncs, nb)]
            sl = slice(k * SLAB, (k + 1) * SLAB)
            o_ref[sl, :] = h1_ref[sl, :] + mod[:, 5 * d:6 * d] * _rms(acc_ref[sl, :], npost_ref[...])

    @pl.when(i == 0)
    def _():
        prologue(h0_ref, g0_ref, s0_ref, r0_ref, 0, h1a_ref, hba_ref)

    nxt = jnp.minimum(i + 1, n - 1)

    @pl.when(i % 2 == 0)
    def _():
        prologue(hn_ref, gn_ref, sn_ref, rn_ref, nxt, h1b_ref, hbb_ref)
        ffn(h1a_ref, hba_ref)

    @pl.when(i % 2 == 1)
    def _():
        prologue(hn_ref, gn_ref, sn_ref, rn_ref, nxt, h1a_ref, hba_ref)
        ffn(h1b_ref, hbb_ref)


def _mixffn_call(h, mod, gla, ssd, ret, wo, nmix, npre, w13, w2, npost, *, layer, spb, ncs, nb):
    r, d = h.shape
    tm = TM_DENSE
    nt = r // tm
    kern = functools.partial(_mixffn_kernel, tm=tm, spb=spb, ncs=ncs, nb=nb)
    row = lambda i: (i, 0)
    first = lambda i: (0, 0)
    ahead = lambda i: (jnp.minimum(i + 1, nt - 1), 0)
    const = lambda i: (0, 0)

    def resident(w):
        zeros = (0,) * (w.ndim - 1)
        return pl.BlockSpec((None,) + w.shape[1:], lambda i: (layer,) + zeros, pipeline_mode=pl.Buffered(1))

    tiles = lambda imap: [pl.BlockSpec((tm, d), imap), pl.BlockSpec((tm, GLA_W), imap),
                          pl.BlockSpec((tm, SSD_W), imap), pl.BlockSpec((tm, RET_W), imap)]
    return pl.pallas_call(
        kern,
        grid=(nt,),
        in_specs=tiles(first) + tiles(ahead) + [
            pl.BlockSpec(mod.shape, lambda i: (0, 0, 0)),
            resident(wo),
            pl.BlockSpec((1, d), const),
            pl.BlockSpec((1, d), const),
            resident(w13),
            resident(w2),
            pl.BlockSpec((1, d), const),
        ],
        out_specs=pl.BlockSpec((tm, d), row),
        out_shape=jax.ShapeDtypeStruct((r, d), F32),
        scratch_shapes=[pltpu.VMEM((tm, d), F32), pltpu.VMEM((tm, d), F32),
                        pltpu.VMEM((tm, d), BF16), pltpu.VMEM((tm, d), BF16), pltpu.VMEM((tm, d), F32)],
        compiler_params=pltpu.CompilerParams(
            dimension_semantics=("arbitrary",), vmem_limit_bytes=VMEM_LIMIT),
        name="mix_ffn",
    )(h, gla, ssd, ret, h, gla, ssd, ret, mod, wo, nmix, npre, w13, w2, npost)


def _tri_pair(c):
    i = jnp.arange(c)[:, None]
    j = jnp.arange(c)[None, :]
    return jnp.stack([(j <= i), (j >= i)]).astype(BF16)


def _gla_cum_pair(cg, sb):
    i = jnp.arange(cg)[:, None]
    j = jnp.arange(cg)[None, :]
    same = (i // sb) == (j // sb)
    fwd = jnp.concatenate([same & (j <= i), same], axis=0)
    bwd = jnp.concatenate([same & (j >= i), same], axis=0)
    return jnp.stack([fwd, bwd]).astype(BF16)


def _constants():
    lane256 = jnp.arange(256)
    consts = {
        "gla_cum": _gla_cum_pair(GLA_BLOCK, GLA_SUB),
        "tri_ssd": _tri_pair(SSD_CHUNK),
        "avg256": (lane256[:, None] // 64 == lane256[None, :] // 64).astype(BF16),
        "gla_e2": (jnp.arange(GLA_KDIM)[:, None] // GLA_HEAD_K == lane256[None, :] // GLA_HEAD_V).astype(BF16),
        "gla_bdm": (lane256[:, None] // GLA_HEAD_V == jnp.arange(GLA_KDIM)[None, :] // GLA_HEAD_K).astype(F32),
        "ssd_x": (jnp.arange(LANES)[:, None] - SMALL_DT_LANE
                  == jnp.arange(SSD_W)[None, :] // SSD_HEAD_DIM).astype(BF16),
        "hmask": (jnp.arange(8)[:, None] == lane256[None, :] // 64).astype(F32),
    }
    return consts


def _rope_tables(t_lat, nctx):
    rows = t_lat // GRID_W
    row = jnp.repeat(jnp.arange(rows), GRID_W).astype(F32)
    col = jnp.tile(jnp.arange(GRID_W), rows).astype(F32)
    n_freq = RET_HEAD_DIM // 4
    inv_freq = ROPE_BASE ** (-jnp.arange(n_freq, dtype=F32) / n_freq)
    ang = jnp.concatenate([row[:, None] * inv_freq, col[:, None] * inv_freq], axis=-1)
    cos = jnp.cos(ang)
    sin = jnp.sin(ang)
    cos_t = jnp.concatenate([cos, cos, cos, cos], axis=-1)
    sin_t = jnp.concatenate([-sin, sin, -sin, sin], axis=-1)
    cos_t = jnp.concatenate([jnp.ones((nctx, LANES), F32), cos_t], axis=0)
    sin_t = jnp.concatenate([jnp.zeros((nctx, LANES), F32), sin_t], axis=0)
    return cos_t, sin_t


def kernel(x, c, ctx, c_ctx, ada_w, ada_b, norm_mix_pre, norm_mix_post, norm_ffn_pre, norm_ffn_post,
           w_in, w_out, gla_gate_up, gla_gate_b, gla_norm, ssd_conv_w, ssd_conv_b, ssd_dt_bias,
           ssd_a_log, ssd_d, ssd_norm, ret_norm, ffn_w13, ffn_w2):
    nb, t_lat, d = x.shape
    nctx = ctx.shape[1]
    depth = ada_w.shape[0]
    rb = nctx + t_lat
    r = nb * rb
    spb = rb // SLAB
    ncs = nctx // SLAB
    assert d == D_MODEL and nb + 1 <= 8
    assert nctx % SLAB == 0 and t_lat % SLAB == 0 and r % TM_DENSE == 0
    assert all(n % s == 0 for n in (nctx, t_lat) for s in (SCAN_STEP_ROWS, GLA_BLOCK))

    consts = _constants()
    cos_t, sin_t = _rope_tables(t_lat, nctx)

    cvec = jnp.zeros((8, d), F32).at[:nb].set(c).at[nb].set(c_ctx)
    mods = _ada_call(cvec, ada_w, ada_b)

    s1, s2 = GLA_COLS, GLA_COLS + SSD_COLS
    lr0 = 2 * GLA_KDIM + 2 * GLA_W
    dt0 = s1 + SSD_W + SSD_CONV_CH
    wm = jnp.concatenate([w_in[:, :, 0:lr0], w_in[:, :, s1:dt0], w_in[:, :, s2:]], axis=-1).astype(BF16)
    zpad = jnp.zeros((depth, d, LANES - GLA_RANK - SSD_HEADS), F32)
    ws = jnp.concatenate(
        [w_in[:, :, lr0:lr0 + GLA_RANK], w_in[:, :, dt0:dt0 + SSD_HEADS], zpad,
         w_in[:, :, lr0 + GLA_RANK:lr0 + 2 * GLA_RANK], w_in[:, :, dt0 + SSD_HEADS:dt0 + 2 * SSD_HEADS], zpad],
        axis=-1).astype(BF16)
    wo = w_out.astype(BF16)
    w13 = ffn_w13.astype(BF16)
    w2 = ffn_w2.astype(BF16)

    wg = jnp.zeros((depth, 2, LANES, GLA_KDIM), F32).at[:, :, :GLA_RANK, :].set(gla_gate_up)
    gb = gla_gate_b.reshape(depth, 2, 1, GLA_KDIM)
    lane_dt = slice(SMALL_DT_LANE, SMALL_DT_LANE + SSD_HEADS)
    dtb = jnp.zeros((depth, 2, 1, LANES), F32).at[:, :, 0, lane_dt].set(ssd_dt_bias)
    alog = jnp.zeros((depth, 2, 1, LANES), F32).at[:, :, 0, lane_dt].set(ssd_a_log)
    dsk = jnp.repeat(ssd_d, SSD_HEAD_DIM, axis=-1).reshape(depth, 1, SSD_W)
    conv_w = jnp.zeros((depth, 8, SSD_CONV_CH), F32).at[:, :SSD_CONV_W].set(ssd_conv_w)

    h = jnp.concatenate([ctx, x], axis=1).reshape(r, d)
    meta = dict(spb=spb, ncs=ncs, nb=nb)
    row1 = lambda a: a.reshape(1, -1)
    for l in range(depth):
        mod = mods[l].reshape(8, 1, 6 * d)
        p_gla, p_z, xconv, p_ret, small_f, small_b = _inproj_call(
            h, mod, row1(norm_mix_pre[l]), wm, ws, conv_w, row1(ssd_conv_b[l]), layer=l, **meta)
        small = (small_f.reshape(nb, rb, LANES), small_b.reshape(nb, rb, LANES))
        xconv = xconv.reshape(nb, rb, SSD_CONV_CH)
        p_gla = p_gla.reshape(nb, rb, 768)
        p_z = p_z.reshape(nb, rb, SSD_W)
        p_ret = p_ret.reshape(nb, rb, 4 * RET_W)
        gla_f = _gla_call(0, p_gla, small[0], None, wg[l, 0], gb[l, 0], None, consts, nctx=nctx)
        gla = _gla_call(1, p_gla, small[1], gla_f, wg[l, 1], gb[l, 1], row1(gla_norm[l]), consts, nctx=nctx)
        ssd_f = _ssd_call(0, xconv, small[0], None, None, dtb[l, 0], alog[l, 0], None, None, consts, nctx=nctx)
        ssd = _ssd_call(1, xconv, small[1], p_z, ssd_f, dtb[l, 1], alog[l, 1], dsk[l], row1(ssd_norm[l]),
                        consts, nctx=nctx)
        ret_f = _ret_call(0, p_ret, cos_t, sin_t, None, None, consts, nctx=nctx)
        ret = _ret_call(1, p_ret, cos_t, sin_t, ret_f, row1(ret_norm[l]), consts, nctx=nctx)
        h = _mixffn_call(h, mod, gla.reshape(r, GLA_W), ssd.reshape(r, SSD_W), ret.reshape(r, RET_W),
                         wo, row1(norm_mix_post[l]), row1(norm_ffn_pre[l]), w13, w2, row1(norm_ffn_post[l]),
                         layer=l, **meta)
    return h.reshape(nb, rb, d)[:, nctx:, :]
```

```python
import functools
import math

import jax
import jax.numpy as jnp
from jax import lax
from jax.experimental import pallas as pl
from jax.experimental.pallas import tpu as pltpu

F32 = jnp.float32
BF16 = jnp.bfloat16

D_MODEL = 1024
GRID_W = 64
RMS_EPS = 1e-6
GLA_W = 256
GLA_HEADS = 4
GLA_HEAD_V = 64
GLA_HEAD_K = 32
GLA_KDIM = 128
GLA_RANK = 16
GLA_GATE_TAU = 16.0
SSD_W = 512
SSD_HEADS = 8
SSD_HEAD_DIM = 64
SSD_GROUPS = 2
SSD_STATE = 128
SSD_CONV_W = 5
SSD_CONV_CH = 1024
RET_W = 256
RET_HEADS = 4
RET_HEAD_DIM = 64
ROPE_BASE = 10000.0
GLA_COLS = 800
SSD_COLS = 1552
FFN_HIDDEN = 2816

LANES = 128
SLAB = 256
TM_DENSE = 512
FFN_CHUNK = 256
GLA_SUB = 16
GLA_BLOCK = 256
SSD_CHUNK = 128
RET_CHUNK = 128
SCAN_STEP_ROWS = 256
SMALL_DT_LANE = 16
VMEM_LIMIT = 56 * 1024 * 1024


def _silu(x):
    return x * jax.nn.sigmoid(x)


def _dot(a, b):
    return jnp.dot(a, b, preferred_element_type=F32)


def _dot_nt(a, b):
    return lax.dot_general(a, b, (((1,), (1,)), ((), ())), preferred_element_type=F32)


def _dot_tn(a, b):
    return lax.dot_general(a, b, (((0,), (0,)), ((), ())), preferred_element_type=F32)


def _split_bf16(x, terms):
    parts = []
    r = x
    for _ in range(terms):
        hi = r.astype(BF16)
        parts.append(hi)
        r = r - hi.astype(F32)
    return parts


def _dot_sel_rhs(m01, x, terms=3):
    acc = None
    for p in _split_bf16(x, terms):
        t = _dot(m01, p)
        acc = t if acc is None else acc + t
    return acc


def _dot_sel_lhs(x, m01, terms=3):
    acc = None
    for p in _split_bf16(x, terms):
        t = _dot(p, m01)
        acc = t if acc is None else acc + t
    return acc


def _dot3(a, b):
    a_hi = a.astype(BF16)
    a_lo = (a - a_hi.astype(F32)).astype(BF16)
    b_hi = b.astype(BF16)
    b_lo = (b - b_hi.astype(F32)).astype(BF16)
    return _dot(a_hi, b_hi) + _dot(a_lo, b_hi) + _dot(a_hi, b_lo)


def _mod_row(slab, spb, ncs, nb):
    return jnp.where(slab % spb < ncs, nb, slab // spb)


def _rms(x, w):
    return x * lax.rsqrt(jnp.mean(x * x, axis=-1, keepdims=True) + RMS_EPS) * w


def _ada_kernel(cv_ref, w_ref, b_ref, o_ref):
    s = _silu(cv_ref[...]).astype(BF16)
    o_ref[...] = _dot(s, w_ref[...].astype(BF16)) + b_ref[...]


def _ada_call(cvec, ada_w, ada_b):
    depth, d, n6 = ada_w.shape
    tn = 2048
    return pl.pallas_call(
        _ada_kernel,
        grid=(depth, n6 // tn),
        in_specs=[
            pl.BlockSpec((8, d), lambda l, j: (0, 0)),
            pl.BlockSpec((None, d, tn), lambda l, j: (l, 0, j)),
            pl.BlockSpec((None, 1, tn), lambda l, j: (l, 0, j)),
        ],
        out_specs=pl.BlockSpec((None, 8, tn), lambda l, j: (l, 0, j)),
        out_shape=jax.ShapeDtypeStruct((depth, 8, n6), F32),
        compiler_params=pltpu.CompilerParams(
            dimension_semantics=("arbitrary", "arbitrary"), vmem_limit_bytes=VMEM_LIMIT),
        name="ada_mod",
    )(cvec, ada_w, ada_b.reshape(depth, 1, n6))


def _inproj_kernel(h_ref, hp_ref, hn_ref, mod_ref, nw_ref, wm_ref, ws_ref, cw_ref, cb_ref,
                   gla_ref, z_ref, xc_ref, ret_ref, smf_ref, smb_ref, yb_ref, scr_ref, *, tm, spb, ncs, nb):
    i = pl.program_id(0)
    d = D_MODEL
    nsl = tm // SLAB
    x0, x1 = 1280, 2304

    def normed(x, mrow):
        return _rms(x, nw_ref[...]) * (1.0 + mod_ref[mrow][:, d:2 * d]) + mod_ref[mrow][:, 0:d]

    mrows = [_mod_row(i * nsl + k, spb, ncs, nb) for k in range(nsl)]
    for k in range(nsl):
        yb_ref[k * SLAB:(k + 1) * SLAB, :] = normed(h_ref[k * SLAB:(k + 1) * SLAB, :], mrows[k]).astype(BF16)
    yb = yb_ref[...]

    yh = jnp.concatenate([normed(hp_ref[...], mrows[0]), normed(hn_ref[...], mrows[nsl - 1])], axis=0)
    halo = _dot(yh.astype(BF16), wm_ref[:, x0:x1])
    xbc = _dot(yb, wm_ref[:, x0:x1])
    gla_ref[...] = _dot(yb, wm_ref[:, 0:768])
    z_ref[...] = _dot(yb, wm_ref[:, 768:x0])
    ret_ref[...] = _dot(yb, wm_ref[:, x1:3328])
    smf_ref[...] = _dot(yb, ws_ref[:, 0:LANES])
    smb_ref[...] = _dot(yb, ws_ref[:, LANES:2 * LANES])
    pad = (SSD_CONV_W - 1) // 2
    for k in range(nsl):
        pos = (i * nsl + k) % spb
        first = jnp.logical_or(pos == 0, pos == ncs)
        last = jnp.logical_or(pos == ncs - 1, pos == spb - 1)
        before = halo[0:8] if k == 0 else xbc[k * SLAB - 8:k * SLAB, :]
        after = halo[8:16] if k == nsl - 1 else xbc[(k + 1) * SLAB:(k + 1) * SLAB + 8, :]
        scr_ref[0:8, :] = jnp.where(first, 0.0, before)
        scr_ref[8:8 + SLAB, :] = xbc[k * SLAB:(k + 1) * SLAB, :]
        scr_ref[8 + SLAB:16 + SLAB, :] = jnp.where(last, 0.0, after)
        ext = scr_ref[...]
        acc = cb_ref[...] + cw_ref[pad:pad + 1, :] * ext[8:8 + SLAB]
        for t in range(SSD_CONV_W):
            if t != pad:
                shifted = pltpu.roll(ext, (pad - t) % (SLAB + 16), 0)[8:8 + SLAB]
                acc = acc + cw_ref[t:t + 1, :] * shifted
        xc_ref[k * SLAB:(k + 1) * SLAB, :] = _silu(acc)


def _layer_spec(w, layer):
    zeros = (0,) * (w.ndim - 1)
    return pl.BlockSpec((None,) + w.shape[1:], lambda i: (layer,) + zeros)


def _inproj_call(h, mod, nw, wm, ws, cw, cb, *, layer, spb, ncs, nb):
    r, d = h.shape
    tm = TM_DENSE
    kern = functools.partial(_inproj_kernel, tm=tm, spb=spb, ncs=ncs, nb=nb)
    row = lambda i: (i, 0)
    const = lambda i: (0, 0)
    per = tm // 8
    return pl.pallas_call(
        kern,
        grid=(r // tm,),
        in_specs=[
            pl.BlockSpec((tm, d), row),
            pl.BlockSpec((8, d), lambda i: (jnp.maximum(i * per - 1, 0), 0)),
            pl.BlockSpec((8, d), lambda i: (jnp.minimum((i + 1) * per, r // 8 - 1), 0)),
            pl.BlockSpec(mod.shape, lambda i: (0, 0, 0)),
            pl.BlockSpec((1, d), const),
            _layer_spec(wm, layer),
            _layer_spec(ws, layer),
            _layer_spec(cw, layer),
            pl.BlockSpec(cb.shape, const),
        ],
        out_specs=[
            pl.BlockSpec((tm, 768), row),
            pl.BlockSpec((tm, 512), row),
            pl.BlockSpec((tm, 1024), row),
            pl.BlockSpec((tm, 1024), row),
            pl.BlockSpec((tm, LANES), row),
            pl.BlockSpec((tm, LANES), row),
        ],
        out_shape=[
            jax.ShapeDtypeStruct((r, 768), F32),
            jax.ShapeDtypeStruct((r, 512), F32),
            jax.ShapeDtypeStruct((r, 1024), F32),
            jax.ShapeDtypeStruct((r, 1024), F32),
            jax.ShapeDtypeStruct((r, LANES), F32),
            jax.ShapeDtypeStruct((r, LANES), F32),
        ],
        scratch_shapes=[pltpu.VMEM((tm, d), BF16), pltpu.VMEM((SLAB + 16, SSD_CONV_CH), F32)],
        compiler_params=pltpu.CompilerParams(
            dimension_semantics=("arbitrary",), vmem_limit_bytes=VMEM_LIMIT),
        name="in_proj",
    )(h, h, h, mod, nw, wm, ws, cw, cb)


def _chunk_of(direction, n, ncc, nch):
    if direction == 0:
        return n
    return jnp.where(n < ncc, ncc - 1 - n, nch + ncc - 1 - n)


def _scan_params():
    return pltpu.CompilerParams(dimension_semantics=("arbitrary",), vmem_limit_bytes=VMEM_LIMIT)


def _gla_kernel(*refs, direction, nb, cg):
    if direction == 0:
        (p_ref, sm_ref, wg_ref, gb_ref, cum_ref, e2_ref, bdm_ref, o_ref, st_ref) = refs
    else:
        (p_ref, sm_ref, of_ref, wg_ref, gb_ref, cum_ref, e2_ref, bdm_ref, avg_ref, nw_ref,
         o_ref, st_ref) = refs
    sb = GLA_SUB
    nsb = cg // sb

    @pl.when(pl.program_id(0) == 0)
    def _():
        st_ref[...] = jnp.zeros_like(st_ref)

    half = 8
    rows8 = lax.broadcasted_iota(jnp.int32, (half, GLA_KDIM), 0)
    masks8 = [(rows8 >= j) if direction == 0 else (rows8 <= j) for j in range(half)]

    def half_state(hf, j):
        if hf == j // half:
            return "part"
        kept = hf > j // half if direction == 0 else hf < j // half
        return "full" if kept else "zero"

    pre = []
    for b in range(nb):
        z = _dot3(sm_ref[b], wg_ref[...]) + gb_ref[...]
        logg = (jnp.minimum(z, 0.0) - jnp.log1p(jnp.exp(-jnp.abs(z)))) * (1.0 / GLA_GATE_TAU)
        cums = _dot_sel_rhs(cum_ref[...], logg)
        gs_all = cums[0:cg]
        gl = cums[cg:2 * cg]
        qs_all = p_ref[b, :, 0:GLA_KDIM] * (GLA_HEAD_K ** -0.5)
        k_all = p_ref[b, :, GLA_KDIM:2 * GLA_KDIM]
        v_all = p_ref[b, :, 2 * GLA_KDIM:2 * GLA_KDIM + GLA_W]
        pre.append(dict(
            gs=gs_all, egl=jnp.exp(gl), qs=qs_all, k=k_all, v=v_all,
            qt=(qs_all * jnp.exp(gs_all)).astype(BF16),
            kk=(k_all * jnp.exp(gl - gs_all)).astype(BF16),
            vb=v_all.astype(BF16)))

    sts = [st_ref[b] for b in range(nb)]
    o_parts = [[None] * nsb for _ in range(nb)]
    for it in range(nsb):
        a = it if direction == 0 else nsb - 1 - it
        sl = slice(a * sb, (a + 1) * sb)
        for b in range(nb):
            pb = pre[b]
            gs, qs, kb, vb = pb["gs"][sl], pb["qs"][sl], pb["k"][sl], pb["v"][sl]
            inter = _dot_nt(pb["qt"][sl], sts[b].astype(BF16))
            parts = []
            for j in range(sb):
                halves = []
                for hf in range(sb // half):
                    rs = slice(hf * half, (hf + 1) * half)
                    if half_state(hf, j) == "zero":
                        halves.append(jnp.zeros((half, GLA_KDIM), F32))
                        continue
                    dec = jnp.exp(gs[rs] - gs[j:j + 1, :])
                    if half_state(hf, j) == "part":
                        dec = jnp.where(masks8[j % half], dec, 0.0)
                    halves.append(dec * qs[rs] * kb[j:j + 1, :])
                parts.append(jnp.concatenate(halves, axis=0).astype(BF16))
            sx = _dot(jnp.concatenate(parts, axis=0), e2_ref[...])
            acc = [None] * (sb // half)
            for j in range(sb):
                for hf in range(sb // half):
                    if half_state(hf, j) == "zero":
                        continue
                    term = sx[j * sb + hf * half:j * sb + (hf + 1) * half] * vb[j:j + 1, :]
                    acc[hf] = term if acc[hf] is None else acc[hf] + term
            o_parts[b][a] = jnp.concatenate(acc, axis=0) + inter
            sts[b] = (sts[b] * pb["egl"][a * sb:a * sb + 1, :]
                      + _dot_tn(pb["vb"][sl], pb["kk"][sl]) * bdm_ref[...])
    for b in range(nb):
        st_ref[b] = sts[b]
    outs = [jnp.concatenate(o_parts[b], axis=0) for b in range(nb)]
    if direction == 0:
        for b in range(nb):
            o_ref[b] = outs[b]
    else:
        tots = [of_ref[b] + outs[b] for b in range(nb)]
        mss = [_dot_sel_lhs(tots[b] * tots[b], avg_ref[...], 2) * (1.0 / GLA_HEAD_V) for b in range(nb)]
        for b in range(nb):
            y = tots[b] * lax.rsqrt(mss[b] + RMS_EPS) * nw_ref[...]
            r = p_ref[b, :, 2 * GLA_KDIM + GLA_W:2 * GLA_KDIM + 2 * GLA_W]
            o_ref[b] = (y * _silu(r)).astype(BF16)


def _gla_call(direction, p_gla, small_d, o_f, wg_d, gb_d, nw, consts, *, nctx):
    nb, rb, _ = p_gla.shape
    cg = GLA_BLOCK
    nch, ncc = rb // cg, nctx // cg
    kern = functools.partial(_gla_kernel, direction=direction, nb=nb, cg=cg)
    chunk = lambda n: (0, _chunk_of(direction, n, ncc, nch), 0)
    const2 = lambda n: (0, 0)
    blk = lambda w: pl.BlockSpec((nb, cg, w), chunk)
    full = lambda a: pl.BlockSpec(a.shape, const2)
    cum = consts["gla_cum"][direction]
    ins = [p_gla, small_d]
    specs = [blk(768), blk(LANES)]
    if direction == 1:
        ins.append(o_f)
        specs.append(blk(GLA_W))
    tail = [wg_d, gb_d, cum, consts["gla_e2"], consts["gla_bdm"]]
    if direction == 1:
        tail += [consts["avg256"], nw]
    ins += tail
    specs += [full(a) for a in tail]
    scratch = [pltpu.VMEM((nb, GLA_W, GLA_KDIM), F32)]
    return pl.pallas_call(
        kern,
        grid=(nch,),
        in_specs=specs,
        out_specs=blk(GLA_W),
        out_shape=jax.ShapeDtypeStruct((nb, rb, GLA_W), F32 if direction == 0 else BF16),
        scratch_shapes=scratch,
        compiler_params=_scan_params(),
        name="gla_fwd" if direction == 0 else "gla_bwd",
    )(*ins)


def _ssd_kernel(*refs, direction, nb, c, nck):
    if direction == 0:
        (xc_ref, sm_ref, dtb_ref, alog_ref, tri_ref, x_ref, hm_ref, o_ref, st_ref) = refs
    else:
        (xc_ref, sm_ref, z_ref, of_ref, dtb_ref, alog_ref, tri_ref, x_ref, hm_ref, dsk_ref, nw_ref,
         o_ref, st_ref) = refs
    hpg = SSD_HEADS // SSD_GROUPS
    gw = hpg * SSD_HEAD_DIM

    @pl.when(pl.program_id(0) == 0)
    def _():
        st_ref[...] = jnp.zeros_like(st_ref)

    ii = lax.broadcasted_iota(jnp.int32, (c, c), 0)
    jj = lax.broadcasted_iota(jnp.int32, (c, c), 1)
    tri_mask = (ii >= jj) if direction == 0 else (ii <= jj)
    neg_a = -jnp.exp(alog_ref[...])

    lane_lo = lax.broadcasted_iota(jnp.int32, (c, LANES), 1) < SSD_HEAD_DIM
    hm_b = hm_ref[...].astype(BF16)
    last = c - 1 if direction == 0 else 0
    d0, d1 = SMALL_DT_LANE, SMALL_DT_LANE + SSD_HEADS

    order = list(range(nck)) if direction == 0 else list(range(nck - 1, -1, -1))
    rows = {ci: slice(ci * c, (ci + 1) * c) for ci in order}
    units = [(ci, b) for ci in order for b in range(nb)]
    upairs = [(ci, b, grp) for ci, b in units for grp in range(SSD_GROUPS)]
    dts = {}
    for ci, b in units:
        raw = sm_ref[b, rows[ci], :] + dtb_ref[...]
        dts[ci, b] = jnp.maximum(raw, 0.0) + jnp.log1p(jnp.exp(-jnp.abs(raw)))
    gs = {u: _dot_sel_rhs(tri_ref[...], dts[u] * neg_a) for u in units}
    gt8s = {u: gs[u].T[d0:d1, :] for u in units}
    dt8s = {u: dts[u].T[d0:d1, :] for u in units}
    w8s = {u: dt8s[u] * jnp.exp(gt8s[u][:, last:last + 1] - gt8s[u]) for u in units}
    gl_xs = {u: _dot_sel_lhs(jnp.broadcast_to(jnp.exp(gs[u][last:last + 1, :]), (8, LANES)), x_ref[...])[0:1, :]
             for u in units}
    xss = {(ci, b): xc_ref[b, rows[ci], 0:SSD_W] for ci, b in units}

    cms, bms, cbs = {}, {}, {}
    for ci, b, grp in upairs:
        b0 = SSD_W + grp * SSD_STATE
        c0 = SSD_W + SSD_GROUPS * SSD_STATE + grp * SSD_STATE
        bms[ci, b, grp] = xc_ref[b, rows[ci], b0:b0 + SSD_STATE]
        cms[ci, b, grp] = xc_ref[b, rows[ci], c0:c0 + SSD_STATE].astype(BF16)
    for p in upairs:
        cbs[p] = _dot_nt(cms[p], bms[p].astype(BF16))

    k_cat, x_bd, e_in, intra = {}, {}, {}, {}
    for ci, b, grp in upairs:
        u = (ci, b)
        bm_t = bms[ci, b, grp].T
        xs_g = xss[u][:, grp * gw:(grp + 1) * gw].astype(BF16)
        a_parts, k_parts, x_parts, col_parts = [], [], [], []
        for hh in range(hpg):
            hd = grp * hpg + hh
            col = jnp.broadcast_to(gs[u][:, d0 + hd:d0 + hd + 1], (c, c))
            dec = jnp.where(tri_mask, jnp.exp(col - gt8s[u][hd:hd + 1, :]), 0.0)
            a_parts.append((cbs[ci, b, grp] * (dec * dt8s[u][hd:hd + 1, :])).astype(BF16))
            k_parts.append((bm_t * w8s[u][hd:hd + 1, :]).astype(BF16))
            x_parts.append(xs_g * hm_b[hh:hh + 1, :])
            col_parts.append(col)
        k_cat[ci, b, grp] = jnp.concatenate(k_parts, axis=1)
        x_bd[ci, b, grp] = jnp.concatenate(x_parts, axis=0)
        e_in[ci, b, grp] = jnp.exp(jnp.concatenate(
            [jnp.where(lane_lo, col_parts[2 * t], col_parts[2 * t + 1]) for t in range(hpg // 2)], axis=1))
        intra[ci, b, grp] = _dot(jnp.concatenate(a_parts, axis=1), x_bd[ci, b, grp])

    sts ={(b, grp): st_ref[b, grp] for b in range(nb) for grp in range(SSD_GROUPS)}
    inter = {}
    for ci in order:
        for b in range(nb):
            for grp in range(SSD_GROUPS):
                inter[ci, b, grp] = e_in[ci, b, grp] * _dot(cms[ci, b, grp], sts[b, grp].astype(BF16))
        for b in range(nb):
            for grp in range(SSD_GROUPS):
                sts[b, grp] = (sts[b, grp] * gl_xs[ci, b][:, grp * gw:(grp + 1) * gw]
                               + _dot(k_cat[ci, b, grp], x_bd[ci, b, grp]))
    for b in range(nb):
        for grp in range(SSD_GROUPS):
            st_ref[b, grp] = sts[b, grp]

    for ci, b in units:
        o = jnp.concatenate([intra[ci, b, grp] + inter[ci, b, grp] for grp in range(SSD_GROUPS)], axis=1)
        if direction == 0:
            o_ref[b, rows[ci], :] = o
        else:
            y = (of_ref[b, rows[ci], :] + o + dsk_ref[...] * xss[ci, b]) * _silu(z_ref[b, rows[ci], :])
            o_ref[b, rows[ci], :] = _rms(y, nw_ref[...]).astype(BF16)


def _ssd_call(direction, xconv, small_d, p_z, o_f, dtb_d, alog_d, dsk, nw, consts, *, nctx):
    nb, rb, _ = xconv.shape
    c = SSD_CHUNK
    assert c == LANES
    nck = SCAN_STEP_ROWS // c
    nch, ncc = rb // SCAN_STEP_ROWS, nctx // SCAN_STEP_ROWS
    kern = functools.partial(_ssd_kernel, direction=direction, nb=nb, c=c, nck=nck)
    chunk = lambda n: (0, _chunk_of(direction, n, ncc, nch), 0)
    const2 = lambda n: (0, 0)
    blk = lambda w: pl.BlockSpec((nb, SCAN_STEP_ROWS, w), chunk)
    full = lambda a: pl.BlockSpec(a.shape, const2)
    ins = [xconv, small_d]
    specs = [blk(SSD_CONV_CH), blk(LANES)]
    if direction == 1:
        ins += [p_z, o_f]
        specs += [blk(SSD_W), blk(SSD_W)]
    tail = [dtb_d, alog_d, consts["tri_ssd"][direction], consts["ssd_x"], consts["hmask"]]
    if direction == 1:
        tail += [dsk, nw]
    ins += tail
    specs += [full(a) for a in tail]
    return pl.pallas_call(
        kern,
        grid=(nch,),
        in_specs=specs,
        out_specs=blk(SSD_W),
        out_shape=jax.ShapeDtypeStruct((nb, rb, SSD_W), F32 if direction == 0 else BF16),
        scratch_shapes=[pltpu.VMEM((nb, SSD_GROUPS, SSD_STATE, 256), F32)],
        compiler_params=_scan_params(),
        name="ssd_fwd" if direction == 0 else "ssd_bwd",
    )(*ins)


def _ret_log_gamma_row(width):
    lane = lax.broadcasted_iota(jnp.int32, (1, width), 1)
    row = jnp.zeros((1, width), F32)
    for h in range(RET_HEADS):
        lg = math.log1p(-(2.0 ** (-5.0 - h)))
        row = jnp.where(lane // RET_HEAD_DIM == h, lg, row)
    return row


def _rope(t, cos, sin_signed):
    lane = lax.broadcasted_iota(jnp.int32, t.shape, 1)
    half = RET_HEAD_DIM // 2
    swapped = jnp.where(lane % RET_HEAD_DIM < half,
                        pltpu.roll(t, LANES - half, 1), pltpu.roll(t, half, 1))
    return t * cos + swapped * sin_signed


def _ret_kernel(*refs, direction, nb, c, nck):
    if direction == 0:
        (p_ref, cos_ref, sin_ref, avg_ref, hm_ref, o_ref, st_ref) = refs
    else:
        (p_ref, cos_ref, sin_ref, of_ref, avg_ref, hm_ref, nw_ref, o_ref, st_ref) = refs
    w = RET_W

    @pl.when(pl.program_id(0) == 0)
    def _():
        st_ref[...] = jnp.zeros_like(st_ref)

    lg_row = _ret_log_gamma_row(w)
    pos = lax.broadcasted_iota(jnp.int32, (c, 1), 0)
    cnt_in = (pos + 1 if direction == 0 else c - pos).astype(F32)
    cnt_st = (c - 1 - pos if direction == 0 else pos).astype(F32)
    e_in = jnp.exp(cnt_in * lg_row)
    e_st = jnp.exp(cnt_st * lg_row)
    e_all = jnp.exp(float(c) * lg_row)
    ii = lax.broadcasted_iota(jnp.int32, (c, c), 0)
    jj = lax.broadcasted_iota(jnp.int32, (c, c), 1)
    dist = (ii - jj) if direction == 0 else (jj - ii)
    dec_parts = []
    for h in range(RET_HEADS):
        lg = math.log1p(-(2.0 ** (-5.0 - h)))
        dec_parts.append(jnp.where(dist >= 0, jnp.exp(dist.astype(F32) * lg), 0.0))
    dec = jnp.concatenate(dec_parts, axis=1)
    bd_mask = avg_ref[...].astype(F32)

    order = list(range(nck)) if direction == 0 else list(range(nck - 1, -1, -1))
    rows = {ci: slice(ci * c, (ci + 1) * c) for ci in order}
    units = [(ci, b) for ci in order for b in range(nb)]
    hm_b = hm_ref[...].astype(BF16)
    qbs, kbs, vbs, kes = {}, {}, {}, {}
    for ci, b in units:
        cos = cos_ref[rows[ci], :]
        sin = sin_ref[rows[ci], :]
        q_halves, k_halves = [], []
        for hf in range(w // LANES):
            qh = p_ref[b, rows[ci], hf * LANES:(hf + 1) * LANES] * (RET_HEAD_DIM ** -0.5)
            kh = p_ref[b, rows[ci], w + hf * LANES:w + (hf + 1) * LANES]
            q_halves.append(_rope(qh, cos, sin))
            k_halves.append(_rope(kh, cos, sin))
        k = jnp.concatenate(k_halves, axis=1)
        qbs[ci, b] = jnp.concatenate(q_halves, axis=1).astype(BF16)
        kbs[ci, b] = k.astype(BF16)
        kes[ci, b] = (k * e_st).astype(BF16)
        vbs[ci, b] = p_ref[b, rows[ci], 2 * w:3 * w].astype(BF16)
    k_bd = {u: jnp.concatenate([kbs[u] * hm_b[h:h + 1, :] for h in range(RET_HEADS)], axis=0) for u in units}
    v_bd = {u: jnp.concatenate([vbs[u] * hm_b[h:h + 1, :] for h in range(RET_HEADS)], axis=0) for u in units}
    scores = {u: (_dot_nt(qbs[u], k_bd[u]) * dec).astype(BF16) for u in units}
    intra = {u: _dot(scores[u], v_bd[u]) for u in units}
    sts = [st_ref[b] for b in range(nb)]
    outs = {}
    for ci in order:
        for b in range(nb):
            outs[ci, b] = intra[ci, b] + e_in * _dot(qbs[ci, b], sts[b].astype(BF16))
        for b in range(nb):
            sts[b] = sts[b] * e_all + _dot_tn(kes[ci, b], vbs[ci, b]) * bd_mask
    for b in range(nb):
        st_ref[b] = sts[b]
    if direction == 0:
        for ci, b in units:
            o_ref[b, rows[ci], :] = outs[ci, b]
    else:
        tots = {(ci, b): of_ref[b, rows[ci], :] + outs[ci, b] for ci, b in units}
        xcs = {u: tots[u] - _dot_sel_lhs(tots[u], avg_ref[...], 2) * (1.0 / RET_HEAD_DIM) for u in units}
        vrs = {u: _dot_sel_lhs(xcs[u] * xcs[u], avg_ref[...], 2) * (1.0 / RET_HEAD_DIM) for u in units}
        for ci, b in units:
            y = xcs[ci, b] * lax.rsqrt(vrs[ci, b] + RMS_EPS) * nw_ref[...]
            o_ref[b, rows[ci], :] = (y * _silu(p_ref[b, rows[ci], 3 * w:4 * w])).astype(BF16)


def _ret_call(direction, p_ret, cos_t, sin_t, o_f, nw, consts, *, nctx):
    nb, rb, _ = p_ret.shape
    c = RET_CHUNK
    step = SCAN_STEP_ROWS
    nch, ncc = rb // step, nctx // step
    kern = functools.partial(_ret_kernel, direction=direction, nb=nb, c=c, nck=step // c)
    chunk = lambda n: (0, _chunk_of(direction, n, ncc, nch), 0)
    tchunk = lambda n: (_chunk_of(direction, n, ncc, nch), 0)
    const2 = lambda n: (0, 0)
    blk = lambda w: pl.BlockSpec((nb, step, w), chunk)
    full = lambda a: pl.BlockSpec(a.shape, const2)
    ins = [p_ret, cos_t, sin_t]
    specs = [blk(4 * RET_W), pl.BlockSpec((step, LANES), tchunk), pl.BlockSpec((step, LANES), tchunk)]
    if direction == 1:
        ins.append(o_f)
        specs.append(blk(RET_W))
    tail = [consts["avg256"], consts["hmask"]]
    if direction == 1:
        tail.append(nw)
    ins += tail
    specs += [full(a) for a in tail]
    return pl.pallas_call(
        kern,
        grid=(nch,),
        in_specs=specs,
        out_specs=blk(RET_W),
        out_shape=jax.ShapeDtypeStruct((nb, rb, RET_W), F32 if direction == 0 else BF16),
        scratch_shapes=[pltpu.VMEM((nb, RET_W, RET_W), F32)],
        compiler_params=_scan_params(),
        name="ret_fwd" if direction == 0 else "ret_bwd",
    )(*ins)


def _mixffn_kernel(h0_ref, g0_ref, s0_ref, r0_ref, hn_ref, gn_ref, sn_ref, rn_ref, mod_ref,
                   wo_ref, nmix_ref, npre_ref, w13_ref, w2_ref, npost_ref, o_ref,
                   h1a_ref, h1b_ref, hba_ref, hbb_ref, acc_ref, *, tm, spb, ncs, nb):
    i = pl.program_id(0)
    n = pl.num_programs(0)
    d = D_MODEL
    nsl = tm // SLAB
    nck = FFN_HIDDEN // FFN_CHUNK

    def prologue(h_ref, g_ref, s_ref, r_ref, tile, h1_ref, hb_ref):
        mixed = jnp.concatenate([g_ref[...], s_ref[...], r_ref[...]], axis=1)
        m = _dot(mixed, wo_ref[...])
        for k in range(nsl):
            mod = mod_ref[_mod_row(tile * nsl + k, spb, ncs, nb)]
            sl = slice(k * SLAB, (k + 1) * SLAB)
            h1 = h_ref[sl, :] + mod[:, 2 * d:3 * d] * _rms(m[sl, :], nmix_ref[...])
            h1_ref[sl, :] = h1
            y = _rms(h1, npre_ref[...]) * (1.0 + mod[:, 4 * d:5 * d]) + mod[:, 3 * d:4 * d]
            hb_ref[sl, :] = y.astype(BF16)

    def ffn(h1_ref, hb_ref):
        hb = hb_ref[...]
        for cidx in range(nck):
            lo = cidx * FFN_CHUNK
            gate = _dot(hb, w13_ref[:, lo:lo + FFN_CHUNK])
            up = _dot(hb, w13_ref[:, FFN_HIDDEN + lo:FFN_HIDDEN + lo + FFN_CHUNK])
            part = _dot((_silu(gate) * up).astype(BF16), w2_ref[lo:lo + FFN_CHUNK, :])
            if cidx == 0:
                acc_ref[...] = part
            else:
                acc_ref[...] += part
        for k in range(nsl):
            mod = mod_ref[_mod_row(i * nsl + k, spb, ncs, nb)]
            sl = slice(k * SLAB, (k + 1) * SLAB)
            o_ref[sl, :] = h1_ref[sl, :] + mod[:, 5 * d:6 * d] * _rms(acc_ref[sl, :], npost_ref[...])

    @pl.when(i == 0)
    def _():
        prologue(h0_ref, g0_ref, s0_ref, r0_ref, 0, h1a_ref, hba_ref)

    nxt = jnp.minimum(i + 1, n - 1)

    @pl.when(i % 2 == 0)
    def _():
        prologue(hn_ref, gn_ref, sn_ref, rn_ref, nxt, h1b_ref, hbb_ref)
        ffn(h1a_ref, hba_ref)

    @pl.when(i % 2 == 1)
    def _():
        prologue(hn_ref, gn_ref, sn_ref, rn_ref, nxt, h1a_ref, hba_ref)
        ffn(h1b_ref, hbb_ref)


def _mixffn_call(h, mod, gla, ssd, ret, wo, nmix, npre, w13, w2, npost, *, layer, spb, ncs, nb):
    r, d = h.shape
    tm = TM_DENSE
    nt = r // tm
    kern = functools.partial(_mixffn_kernel, tm=tm, spb=spb, ncs=ncs, nb=nb)
    row = lambda i: (i, 0)
    first = lambda i: (0, 0)
    ahead = lambda i: (jnp.minimum(i + 1, nt - 1), 0)
    const = lambda i: (0, 0)

    def resident(w):
        zeros = (0,) * (w.ndim - 1)
        return pl.BlockSpec((None,) + w.shape[1:], lambda i: (layer,) + zeros, pipeline_mode=pl.Buffered(1))

    tiles = lambda imap: [pl.BlockSpec((tm, d), imap), pl.BlockSpec((tm, GLA_W), imap),
                          pl.BlockSpec((tm, SSD_W), imap), pl.BlockSpec((tm, RET_W), imap)]
    return pl.pallas_call(
        kern,
        grid=(nt,),
        in_specs=tiles(first) + tiles(ahead) + [
            pl.BlockSpec(mod.shape, lambda i: (0, 0, 0)),
            resident(wo),
            pl.BlockSpec((1, d), const),
            pl.BlockSpec((1, d), const),
            resident(w13),
            resident(w2),
            pl.BlockSpec((1, d), const),
        ],
        out_specs=pl.BlockSpec((tm, d), row),
        out_shape=jax.ShapeDtypeStruct((r, d), F32),
        scratch_shapes=[pltpu.VMEM((tm, d), F32), pltpu.VMEM((tm, d), F32),
                        pltpu.VMEM((tm, d), BF16), pltpu.VMEM((tm, d), BF16), pltpu.VMEM((tm, d), F32)],
        compiler_params=pltpu.CompilerParams(
            dimension_semantics=("arbitrary",), vmem_limit_bytes=VMEM_LIMIT),
        name="mix_ffn",
    )(h, gla, ssd, ret, h, gla, ssd, ret, mod, wo, nmix, npre, w13, w2, npost)


def _tri_pair(c):
    i = jnp.arange(c)[:, None]
    j = jnp.arange(c)[None, :]
    return jnp.stack([(j <= i), (j >= i)]).astype(BF16)


def _gla_cum_pair(cg, sb):
    i = jnp.arange(cg)[:, None]
    j = jnp.arange(cg)[None, :]
    same = (i // sb) == (j // sb)
    fwd = jnp.concatenate([same & (j <= i), same], axis=0)
    bwd = jnp.concatenate([same & (j >= i), same], axis=0)
    return jnp.stack([fwd, bwd]).astype(BF16)


def _constants():
    lane256 = jnp.arange(256)
    consts = {
        "gla_cum": _gla_cum_pair(GLA_BLOCK, GLA_SUB),
        "tri_ssd": _tri_pair(SSD_CHUNK),
        "avg256": (lane256[:, None] // 64 == lane256[None, :] // 64).astype(BF16),
        "gla_e2": (jnp.arange(GLA_KDIM)[:, None] // GLA_HEAD_K == lane256[None, :] // GLA_HEAD_V).astype(BF16),
        "gla_bdm": (lane256[:, None] // GLA_HEAD_V == jnp.arange(GLA_KDIM)[None, :] // GLA_HEAD_K).astype(F32),
        "ssd_x": (jnp.arange(LANES)[:, None] - SMALL_DT_LANE
                  == jnp.arange(SSD_W)[None, :] // SSD_HEAD_DIM).astype(BF16),
        "hmask": (jnp.arange(8)[:, None] == lane256[None, :] // 64).astype(F32),
    }
    return consts


def _rope_tables(t_lat, nctx):
    rows = t_lat // GRID_W
    row = jnp.repeat(jnp.arange(rows), GRID_W).astype(F32)
    col = jnp.tile(jnp.arange(GRID_W), rows).astype(F32)
    n_freq = RET_HEAD_DIM // 4
    inv_freq = ROPE_BASE ** (-jnp.arange(n_freq, dtype=F32) / n_freq)
    ang = jnp.concatenate([row[:, None] * inv_freq, col[:, None] * inv_freq], axis=-1)
    cos = jnp.cos(ang)
    sin = jnp.sin(ang)
    cos_t = jnp.concatenate([cos, cos, cos, cos], axis=-1)
    sin_t = jnp.concatenate([-sin, sin, -sin, sin], axis=-1)
    cos_t = jnp.concatenate([jnp.ones((nctx, LANES), F32), cos_t], axis=0)
    sin_t = jnp.concatenate([jnp.zeros((nctx, LANES), F32), sin_t], axis=0)
    return cos_t, sin_t


def kernel(x, c, ctx, c_ctx, ada_w, ada_b, norm_mix_pre, norm_mix_post, norm_ffn_pre, norm_ffn_post,
           w_in, w_out, gla_gate_up, gla_gate_b, gla_norm, ssd_conv_w, ssd_conv_b, ssd_dt_bias,
           ssd_a_log, ssd_d, ssd_norm, ret_norm, ffn_w13, ffn_w2):
    nb, t_lat, d = x.shape
    nctx = ctx.shape[1]
    depth = ada_w.shape[0]
    rb = nctx + t_lat
    r = nb * rb
    spb = rb // SLAB
    ncs = nctx // SLAB
    assert d == D_MODEL and nb + 1 <= 8
    assert nctx % SLAB == 0 and t_lat % SLAB == 0 and r % TM_DENSE == 0
    assert all(n % s == 0 for n in (nctx, t_lat) for s in (SCAN_STEP_ROWS, GLA_BLOCK))

    consts = _constants()
    cos_t, sin_t = _rope_tables(t_lat, nctx)

    cvec = jnp.zeros((8, d), F32).at[:nb].set(c).at[nb].set(c_ctx)
    mods = _ada_call(cvec, ada_w, ada_b)

    s1, s2 = GLA_COLS, GLA_COLS + SSD_COLS
    lr0 = 2 * GLA_KDIM + 2 * GLA_W
    dt0 = s1 + SSD_W + SSD_CONV_CH
    wm = jnp.concatenate([w_in[:, :, 0:lr0], w_in[:, :, s1:dt0], w_in[:, :, s2:]], axis=-1).astype(BF16)
    zpad = jnp.zeros((depth, d, LANES - GLA_RANK - SSD_HEADS), F32)
    ws = jnp.concatenate(
        [w_in[:, :, lr0:lr0 + GLA_RANK], w_in[:, :, dt0:dt0 + SSD_HEADS], zpad,
         w_in[:, :, lr0 + GLA_RANK:lr0 + 2 * GLA_RANK], w_in[:, :, dt0 + SSD_HEADS:dt0 + 2 * SSD_HEADS], zpad],
        axis=-1).astype(BF16)
    wo = w_out.astype(BF16)
    w13 = ffn_w13.astype(BF16)
    w2 = ffn_w2.astype(BF16)

    wg = jnp.zeros((depth, 2, LANES, GLA_KDIM), F32).at[:, :, :GLA_RANK, :].set(gla_gate_up)
    gb = gla_gate_b.reshape(depth, 2, 1, GLA_KDIM)
    lane_dt = slice(SMALL_DT_LANE, SMALL_DT_LANE + SSD_HEADS)
    dtb = jnp.zeros((depth, 2, 1, LANES), F32).at[:, :, 0, lane_dt].set(ssd_dt_bias)
    alog = jnp.zeros((depth, 2, 1, LANES), F32).at[:, :, 0, lane_dt].set(ssd_a_log)
    dsk = jnp.repeat(ssd_d, SSD_HEAD_DIM, axis=-1).reshape(depth, 1, SSD_W)
    conv_w = jnp.zeros((depth, 8, SSD_CONV_CH), F32).at[:, :SSD_CONV_W].set(ssd_conv_w)

    h = jnp.concatenate([ctx, x], axis=1).reshape(r, d)
    meta = dict(spb=spb, ncs=ncs, nb=nb)
    row1 = lambda a: a.reshape(1, -1)
    for l in range(depth):
        mod = mods[l].reshape(8, 1, 6 * d)
        p_gla, p_z, xconv, p_ret, small_f, small_b = _inproj_call(
            h, mod, row1(norm_mix_pre[l]), wm, ws, conv_w, row1(ssd_conv_b[l]), layer=l, **meta)
        small = (small_f.reshape(nb, rb, LANES), small_b.reshape(nb, rb, LANES))
        xconv = xconv.reshape(nb, rb, SSD_CONV_CH)
        p_gla = p_gla.reshape(nb, rb, 768)
        p_z = p_z.reshape(nb, rb, SSD_W)
        p_ret = p_ret.reshape(nb, rb, 4 * RET_W)
        gla_f = _gla_call(0, p_gla, small[0], None, wg[l, 0], gb[l, 0], None, consts, nctx=nctx)
        gla = _gla_call(1, p_gla, small[1], gla_f, wg[l, 1], gb[l, 1], row1(gla_norm[l]), consts, nctx=nctx)
        ssd_f = _ssd_call(0, xconv, small[0], None, None, dtb[l, 0], alog[l, 0], None, None, consts, nctx=nctx)
        ssd = _ssd_call(1, xconv, small[1], p_z, ssd_f, dtb[l, 1], alog[l, 1], dsk[l], row1(ssd_norm[l]),
                        consts, nctx=nctx)
        ret_f = _ret_call(0, p_ret, cos_t, sin_t, None, None, consts, nctx=nctx)
        ret = _ret_call(1, p_ret, cos_t, sin_t, ret_f, row1(ret_norm[l]), consts, nctx=nctx)
        h = _mixffn_call(h, mod, gla.reshape(r, GLA_W), ssd.reshape(r, SSD_W), ret.reshape(r, RET_W),
                         wo, row1(norm_mix_post[l]), row1(norm_ffn_pre[l]), w13, w2, row1(norm_ffn_post[l]),
                         layer=l, **meta)
    return h.reshape(nb, rb, d)[:, nctx:, :]
```

```python
import functools
import math

import jax
import jax.numpy as jnp
from jax import lax
from jax.experimental import pallas as pl
from jax.experimental.pallas import tpu as pltpu

F32 = jnp.float32
BF16 = jnp.bfloat16

D_MODEL = 1024
GRID_W = 64
RMS_EPS = 1e-6
GLA_W = 256
GLA_HEADS = 4
GLA_HEAD_V = 64
GLA_HEAD_K = 32
GLA_KDIM = 128
GLA_RANK = 16
GLA_GATE_TAU = 16.0
SSD_W = 512
SSD_HEADS = 8
SSD_HEAD_DIM = 64
SSD_GROUPS = 2
SSD_STATE = 128
SSD_CONV_W = 5
SSD_CONV_CH = 1024
RET_W = 256
RET_HEADS = 4
RET_HEAD_DIM = 64
ROPE_BASE = 10000.0
GLA_COLS = 800
SSD_COLS = 1552
FFN_HIDDEN = 2816

LANES = 128
SLAB = 256
TM_DENSE = 512
FFN_CHUNK = 256
GLA_SUB = 16
GLA_BLOCK = 256
SSD_CHUNK = 128
RET_CHUNK = 128
SCAN_STEP_ROWS = 256
SMALL_DT_LANE = 16
VMEM_LIMIT = 56 * 1024 * 1024


def _silu(x):
    return x * jax.nn.sigmoid(x)


def _dot(a, b):
    return jnp.dot(a, b, preferred_element_type=F32)


def _dot_nt(a, b):
    return lax.dot_general(a, b, (((1,), (1,)), ((), ())), preferred_element_type=F32)


def _dot_tn(a, b):
    return lax.dot_general(a, b, (((0,), (0,)), ((), ())), preferred_element_type=F32)


def _split_bf16(x, terms):
    parts = []
    r = x
    for _ in range(terms):
        hi = r.astype(BF16)
        parts.append(hi)
        r = r - hi.astype(F32)
    return parts


def _dot_sel_rhs(m01, x, terms=3):
    acc = None
    for p in _split_bf16(x, terms):
        t = _dot(m01, p)
        acc = t if acc is None else acc + t
    return acc


def _dot_sel_lhs(x, m01, terms=3):
    acc = None
    for p in _split_bf16(x, terms):
        t = _dot(p, m01)
        acc = t if acc is None else acc + t
    return acc


def _dot3(a, b):
    a_hi = a.astype(BF16)
    a_lo = (a - a_hi.astype(F32)).astype(BF16)
    b_hi = b.astype(BF16)
    b_lo = (b - b_hi.astype(F32)).astype(BF16)
    return _dot(a_hi, b_hi) + _dot(a_lo, b_hi) + _dot(a_hi, b_lo)


def _mod_row(slab, spb, ncs, nb):
    return jnp.where(slab % spb < ncs, nb, slab // spb)


def _rms(x, w):
    return x * lax.rsqrt(jnp.mean(x * x, axis=-1, keepdims=True) + RMS_EPS) * w


def _ada_kernel(cv_ref, w_ref, b_ref, o_ref):
    s = _silu(cv_ref[...]).astype(BF16)
    o_ref[...] = _dot(s, w_ref[...].astype(BF16)) + b_ref[...]


def _ada_call(cvec, ada_w, ada_b):
    depth, d, n6 = ada_w.shape
    tn = 2048
    return pl.pallas_call(
        _ada_kernel,
        grid=(depth, n6 // tn),
        in_specs=[
            pl.BlockSpec((8, d), lambda l, j: (0, 0)),
            pl.BlockSpec((None, d, tn), lambda l, j: (l, 0, j)),
            pl.BlockSpec((None, 1, tn), lambda l, j: (l, 0, j)),
        ],
        out_specs=pl.BlockSpec((None, 8, tn), lambda l, j: (l, 0, j)),
        out_shape=jax.ShapeDtypeStruct((depth, 8, n6), F32),
        compiler_params=pltpu.CompilerParams(
            dimension_semantics=("arbitrary", "arbitrary"), vmem_limit_bytes=VMEM_LIMIT),
        name="ada_mod",
    )(cvec, ada_w, ada_b.reshape(depth, 1, n6))


def _inproj_kernel(h_ref, hp_ref, hn_ref, mod_ref, nw_ref, wm_ref, ws_ref, cw_ref, cb_ref,
                   gla_ref, z_ref, xc_ref, ret_ref, smf_ref, smb_ref, yb_ref, scr_ref, *, tm, spb, ncs, nb):
    i = pl.program_id(0)
    d = D_MODEL
    nsl = tm // SLAB
    x0, x1 = 1280, 2304

    def normed(x, mrow):
        return _rms(x, nw_ref[...]) * (1.0 + mod_ref[mrow][:, d:2 * d]) + mod_ref[mrow][:, 0:d]

    mrows = [_mod_row(i * nsl + k, spb, ncs, nb) for k in range(nsl)]
    for k in range(nsl):
        yb_ref[k * SLAB:(k + 1) * SLAB, :] = normed(h_ref[k * SLAB:(k + 1) * SLAB, :], mrows[k]).astype(BF16)
    yb = yb_ref[...]

    yh = jnp.concatenate([normed(hp_ref[...], mrows[0]), normed(hn_ref[...], mrows[nsl - 1])], axis=0)
    halo = _dot(yh.astype(BF16), wm_ref[:, x0:x1])
    xbc = _dot(yb, wm_ref[:, x0:x1])
    gla_ref[...] = _dot(yb, wm_ref[:, 0:768])
    z_ref[...] = _dot(yb, wm_ref[:, 768:x0])
    ret_ref[...] = _dot(yb, wm_ref[:, x1:3328])
    ps = _dot(yb, ws_ref[...])
    smf_ref[...] = ps[:, 0:LANES]
    smb_ref[...] = ps[:, LANES:2 * LANES]
    pad = (SSD_CONV_W - 1) // 2
    for k in range(nsl):
        pos = (i * nsl + k) % spb
        first = jnp.logical_or(pos == 0, pos == ncs)
        last = jnp.logical_or(pos == ncs - 1, pos == spb - 1)
        before = halo[0:8] if k == 0 else xbc[k * SLAB - 8:k * SLAB, :]
        after = halo[8:16] if k == nsl - 1 else xbc[(k + 1) * SLAB:(k + 1) * SLAB + 8, :]
        scr_ref[0:8, :] = jnp.where(first, 0.0, before)
        scr_ref[8:8 + SLAB, :] = xbc[k * SLAB:(k + 1) * SLAB, :]
        scr_ref[8 + SLAB:16 + SLAB, :] = jnp.where(last, 0.0, after)
        ext = scr_ref[...]
        acc = cb_ref[...] + cw_ref[pad:pad + 1, :] * ext[8:8 + SLAB]
        for t in range(SSD_CONV_W):
            if t != pad:
                shifted = pltpu.roll(ext, (pad - t) % (SLAB + 16), 0)[8:8 + SLAB]
                acc = acc + cw_ref[t:t + 1, :] * shifted
        xc_ref[k * SLAB:(k + 1) * SLAB, :] = _silu(acc)


def _layer_spec(w, layer):
    zeros = (0,) * (w.ndim - 1)
    return pl.BlockSpec((None,) + w.shape[1:], lambda i: (layer,) + zeros)


def _inproj_call(h, mod, nw, wm, ws, cw, cb, *, layer, spb, ncs, nb):
    r, d = h.shape
    tm = TM_DENSE
    kern = functools.partial(_inproj_kernel, tm=tm, spb=spb, ncs=ncs, nb=nb)
    row = lambda i: (i, 0)
    const = lambda i: (0, 0)
    per = tm // 8
    return pl.pallas_call(
        kern,
        grid=(r // tm,),
        in_specs=[
            pl.BlockSpec((tm, d), row),
            pl.BlockSpec((8, d), lambda i: (jnp.maximum(i * per - 1, 0), 0)),
            pl.BlockSpec((8, d), lambda i: (jnp.minimum((i + 1) * per, r // 8 - 1), 0)),
            pl.BlockSpec(mod.shape, lambda i: (0, 0, 0)),
            pl.BlockSpec((1, d), const),
            _layer_spec(wm, layer),
            _layer_spec(ws, layer),
            _layer_spec(cw, layer),
            pl.BlockSpec(cb.shape, const),
        ],
        out_specs=[
            pl.BlockSpec((tm, 768), row),
            pl.BlockSpec((tm, 512), row),
            pl.BlockSpec((tm, 1024), row),
            pl.BlockSpec((tm, 1024), row),
            pl.BlockSpec((tm, LANES), row),
            pl.BlockSpec((tm, LANES), row),
        ],
        out_shape=[
            jax.ShapeDtypeStruct((r, 768), F32),
            jax.ShapeDtypeStruct((r, 512), F32),
            jax.ShapeDtypeStruct((r, 1024), F32),
            jax.ShapeDtypeStruct((r, 1024), F32),
            jax.ShapeDtypeStruct((r, LANES), F32),
            jax.ShapeDtypeStruct((r, LANES), F32),
        ],
        scratch_shapes=[pltpu.VMEM((tm, d), BF16), pltpu.VMEM((SLAB + 16, SSD_CONV_CH), F32)],
        compiler_params=pltpu.CompilerParams(
            dimension_semantics=("arbitrary",), vmem_limit_bytes=VMEM_LIMIT),
        name="in_proj",
    )(h, h, h, mod, nw, wm, ws, cw, cb)


def _chunk_of(direction, n, ncc, nch):
    if direction == 0:
        return n
    return jnp.where(n < ncc, ncc - 1 - n, nch + ncc - 1 - n)


def _scan_params():
    return pltpu.CompilerParams(dimension_semantics=("arbitrary",), vmem_limit_bytes=VMEM_LIMIT)


def _gla_kernel(*refs, direction, nb, cg):
    if direction == 0:
        (p_ref, sm_ref, wg_ref, gb_ref, cum_ref, e2_ref, bdm_ref, o_ref, st_ref) = refs
    else:
        (p_ref, sm_ref, of_ref, wg_ref, gb_ref, cum_ref, e2_ref, bdm_ref, avg_ref, nw_ref,
         o_ref, st_ref) = refs
    sb = GLA_SUB
    nsb = cg // sb

    @pl.when(pl.program_id(0) == 0)
    def _():
        st_ref[...] = jnp.zeros_like(st_ref)

    half = 8
    rows8 = lax.broadcasted_iota(jnp.int32, (half, GLA_KDIM), 0)
    masks8 = [(rows8 >= j) if direction == 0 else (rows8 <= j) for j in range(half)]

    def half_state(hf, j):
        if hf == j // half:
            return "part"
        kept = hf > j // half if direction == 0 else hf < j // half
        return "full" if kept else "zero"

    pre = []
    for b in range(nb):
        z = _dot3(sm_ref[b], wg_ref[...]) + gb_ref[...]
        logg = (jnp.minimum(z, 0.0) - jnp.log1p(jnp.exp(-jnp.abs(z)))) * (1.0 / GLA_GATE_TAU)
        cums = _dot_sel_rhs(cum_ref[...], logg)
        gs_all = cums[0:cg]
        gl = cums[cg:2 * cg]
        qs_all = p_ref[b, :, 0:GLA_KDIM] * (GLA_HEAD_K ** -0.5)
        k_all = p_ref[b, :, GLA_KDIM:2 * GLA_KDIM]
        v_all = p_ref[b, :, 2 * GLA_KDIM:2 * GLA_KDIM + GLA_W]
        pre.append(dict(
            gs=gs_all, egl=jnp.exp(gl), qs=qs_all, k=k_all, v=v_all,
            qt=(qs_all * jnp.exp(gs_all)).astype(BF16),
            kk=(k_all * jnp.exp(gl - gs_all)).astype(BF16),
            vb=v_all.astype(BF16)))

    sts = [st_ref[b] for b in range(nb)]
    o_parts = [[None] * nsb for _ in range(nb)]
    for it in range(nsb):
        a = it if direction == 0 else nsb - 1 - it
        sl = slice(a * sb, (a + 1) * sb)
        for b in range(nb):
            pb = pre[b]
            gs, qs, kb, vb = pb["gs"][sl], pb["qs"][sl], pb["k"][sl], pb["v"][sl]
            inter = _dot_nt(pb["qt"][sl], sts[b].astype(BF16))
            parts = []
            for j in range(sb):
                halves = []
                for hf in range(sb // half):
                    rs = slice(hf * half, (hf + 1) * half)
                    if half_state(hf, j) == "zero":
                        halves.append(jnp.zeros((half, GLA_KDIM), F32))
                        continue
                    dec = jnp.exp(gs[rs] - gs[j:j + 1, :])
                    if half_state(hf, j) == "part":
                        dec = jnp.where(masks8[j % half], dec, 0.0)
                    halves.append(dec * qs[rs] * kb[j:j + 1, :])
                parts.append(jnp.concatenate(halves, axis=0).astype(BF16))
            sx = _dot(jnp.concatenate(parts, axis=0), e2_ref[...])
            acc = [None] * (sb // half)
            for j in range(sb):
                for hf in range(sb // half):
                    if half_state(hf, j) == "zero":
                        continue
                    term = sx[j * sb + hf * half:j * sb + (hf + 1) * half] * vb[j:j + 1, :]
                    acc[hf] = term if acc[hf] is None else acc[hf] + term
            o_parts[b][a] = jnp.concatenate(acc, axis=0) + inter
            sts[b] = (sts[b] * pb["egl"][a * sb:a * sb + 1, :]
                      + _dot_tn(pb["vb"][sl], pb["kk"][sl]) * bdm_ref[...])
    for b in range(nb):
        st_ref[b] = sts[b]
    outs = [jnp.concatenate(o_parts[b], axis=0) for b in range(nb)]
    if direction == 0:
        for b in range(nb):
            o_ref[b] = outs[b]
    else:
        tots = [of_ref[b] + outs[b] for b in range(nb)]
        mss = [_dot_sel_lhs(tots[b] * tots[b], avg_ref[...], 2) * (1.0 / GLA_HEAD_V) for b in range(nb)]
        for b in range(nb):
            y = tots[b] * lax.rsqrt(mss[b] + RMS_EPS) * nw_ref[...]
            r = p_ref[b, :, 2 * GLA_KDIM + GLA_W:2 * GLA_KDIM + 2 * GLA_W]
            o_ref[b] = (y * _silu(r)).astype(BF16)


def _gla_call(direction, p_gla, small_d, o_f, wg_d, gb_d, nw, consts, *, nctx):
    nb, rb, _ = p_gla.shape
    cg = GLA_BLOCK
    nch, ncc = rb // cg, nctx // cg
    kern = functools.partial(_gla_kernel, direction=direction, nb=nb, cg=cg)
    chunk = lambda n: (0, _chunk_of(direction, n, ncc, nch), 0)
    const2 = lambda n: (0, 0)
    blk = lambda w: pl.BlockSpec((nb, cg, w), chunk)
    full = lambda a: pl.BlockSpec(a.shape, const2)
    cum = consts["gla_cum"][direction]
    ins = [p_gla, small_d]
    specs = [blk(768), blk(LANES)]
    if direction == 1:
        ins.append(o_f)
        specs.append(blk(GLA_W))
    tail = [wg_d, gb_d, cum, consts["gla_e2"], consts["gla_bdm"]]
    if direction == 1:
        tail += [consts["avg256"], nw]
    ins += tail
    specs += [full(a) for a in tail]
    scratch = [pltpu.VMEM((nb, GLA_W, GLA_KDIM), F32)]
    return pl.pallas_call(
        kern,
        grid=(nch,),
        in_specs=specs,
        out_specs=blk(GLA_W),
        out_shape=jax.ShapeDtypeStruct((nb, rb, GLA_W), F32 if direction == 0 else BF16),
        scratch_shapes=scratch,
        compiler_params=_scan_params(),
        name="gla_fwd" if direction == 0 else "gla_bwd",
    )(*ins)


def _ssd_kernel(*refs, direction, nb, c, nck):
    if direction == 0:
        (xc_ref, sm_ref, dtb_ref, alog_ref, tri_ref, x_ref, hm_ref, o_ref, st_ref) = refs
    else:
        (xc_ref, sm_ref, z_ref, of_ref, dtb_ref, alog_ref, tri_ref, x_ref, hm_ref, dsk_ref, nw_ref,
         o_ref, st_ref) = refs
    hpg = SSD_HEADS // SSD_GROUPS
    gw = hpg * SSD_HEAD_DIM

    @pl.when(pl.program_id(0) == 0)
    def _():
        st_ref[...] = jnp.zeros_like(st_ref)

    ii = lax.broadcasted_iota(jnp.int32, (c, c), 0)
    jj = lax.broadcasted_iota(jnp.int32, (c, c), 1)
    tri_mask = (ii >= jj) if direction == 0 else (ii <= jj)
    neg_a = -jnp.exp(alog_ref[...])

    lane_lo = lax.broadcasted_iota(jnp.int32, (c, LANES), 1) < SSD_HEAD_DIM
    hm_b = hm_ref[...].astype(BF16)
    last = c - 1 if direction == 0 else 0
    d0, d1 = SMALL_DT_LANE, SMALL_DT_LANE + SSD_HEADS

    order = list(range(nck)) if direction == 0 else list(range(nck - 1, -1, -1))
    rows = {ci: slice(ci * c, (ci + 1) * c) for ci in order}
    units = [(ci, b) for ci in order for b in range(nb)]
    upairs = [(ci, b, grp) for ci, b in units for grp in range(SSD_GROUPS)]
    dts = {}
    for ci, b in units:
        raw = sm_ref[b, rows[ci], :] + dtb_ref[...]
        dts[ci, b] = jnp.maximum(raw, 0.0) + jnp.log1p(jnp.exp(-jnp.abs(raw)))
    gs = {u: _dot_sel_rhs(tri_ref[...], dts[u] * neg_a) for u in units}
    gt8s = {u: gs[u].T[d0:d1, :] for u in units}
    dt8s = {u: dts[u].T[d0:d1, :] for u in units}
    w8s = {u: dt8s[u] * jnp.exp(gt8s[u][:, last:last + 1] - gt8s[u]) for u in units}
    gl_xs = {u: _dot_sel_lhs(jnp.broadcast_to(jnp.exp(gs[u][last:last + 1, :]), (8, LANES)), x_ref[...])[0:1, :]
             for u in units}
    xss = {(ci, b): xc_ref[b, rows[ci], 0:SSD_W] for ci, b in units}

    cms, bms, cbs = {}, {}, {}
    for ci, b, grp in upairs:
        b0 = SSD_W + grp * SSD_STATE
        c0 = SSD_W + SSD_GROUPS * SSD_STATE + grp * SSD_STATE
        bms[ci, b, grp] = xc_ref[b, rows[ci], b0:b0 + SSD_STATE]
        cms[ci, b, grp] = xc_ref[b, rows[ci], c0:c0 + SSD_STATE].astype(BF16)
    for p in upairs:
        cbs[p] = _dot_nt(cms[p], bms[p].astype(BF16))

    k_cat, x_bd, e_in, intra = {}, {}, {}, {}
    for ci, b, grp in upairs:
        u = (ci, b)
        bm_t = bms[ci, b, grp].T
        xs_g = xss[u][:, grp * gw:(grp + 1) * gw].astype(BF16)
        a_parts, k_parts, x_parts, col_parts = [], [], [], []
        for hh in range(hpg):
            hd = grp * hpg + hh
            col = jnp.broadcast_to(gs[u][:, d0 + hd:d0 + hd + 1], (c, c))
            dec = jnp.where(tri_mask, jnp.exp(col - gt8s[u][hd:hd + 1, :]), 0.0)
            a_parts.append((cbs[ci, b, grp] * (dec * dt8s[u][hd:hd + 1, :])).astype(BF16))
            k_parts.append((bm_t * w8s[u][hd:hd + 1, :]).astype(BF16))
            x_parts.append(xs_g * hm_b[hh:hh + 1, :])
            col_parts.append(col)
        k_cat[ci, b, grp] = jnp.concatenate(k_parts, axis=1)
        x_bd[ci, b, grp] = jnp.concatenate(x_parts, axis=0)
        e_in[ci, b, grp] = jnp.exp(jnp.concatenate(
            [jnp.where(lane_lo, col_parts[2 * t], col_parts[2 * t + 1]) for t in range(hpg // 2)], axis=1))
        intra[ci, b, grp] = _dot(jnp.concatenate(a_parts, axis=1), x_bd[ci, b, grp])

    sts ={(b, grp): st_ref[b, grp] for b in range(nb) for grp in range(SSD_GROUPS)}
    inter = {}
    for ci in order:
        for b in range(nb):
            for grp in range(SSD_GROUPS):
                inter[ci, b, grp] = e_in[ci, b, grp] * _dot(cms[ci, b, grp], sts[b, grp].astype(BF16))
        for b in range(nb):
            for grp in range(SSD_GROUPS):
                sts[b, grp] = (sts[b, grp] * gl_xs[ci, b][:, grp * gw:(grp + 1) * gw]
                               + _dot(k_cat[ci, b, grp], x_bd[ci, b, grp]))
    for b in range(nb):
        for grp in range(SSD_GROUPS):
            st_ref[b, grp] = sts[b, grp]

    for ci, b in units:
        o = jnp.concatenate([intra[ci, b, grp] + inter[ci, b, grp] for grp in range(SSD_GROUPS)], axis=1)
        if direction == 0:
            o_ref[b, rows[ci], :] = o
        else:
            y = (of_ref[b, rows[ci], :] + o + dsk_ref[...] * xss[ci, b]) * _silu(z_ref[b, rows[ci], :])
            o_ref[b, rows[ci], :] = _rms(y, nw_ref[...]).astype(BF16)


def _ssd_call(direction, xconv, small_d, p_z, o_f, dtb_d, alog_d, dsk, nw, consts, *, nctx):
    nb, rb, _ = xconv.shape
    c = SSD_CHUNK
    assert c == LANES
    nck = SCAN_STEP_ROWS // c
    nch, ncc = rb // SCAN_STEP_ROWS, nctx // SCAN_STEP_ROWS
    kern = functools.partial(_ssd_kernel, direction=direction, nb=nb, c=c, nck=nck)
    chunk = lambda n: (0, _chunk_of(direction, n, ncc, nch), 0)
    const2 = lambda n: (0, 0)
    blk = lambda w: pl.BlockSpec((nb, SCAN_STEP_ROWS, w), chunk)
    full = lambda a: pl.BlockSpec(a.shape, const2)
    ins = [xconv, small_d]
    specs = [blk(SSD_CONV_CH), blk(LANES)]
    if direction == 1:
        ins += [p_z, o_f]
        specs += [blk(SSD_W), blk(SSD_W)]
    tail = [dtb_d, alog_d, consts["tri_ssd"][direction], consts["ssd_x"], consts["hmask"]]
    if direction == 1:
        tail += [dsk, nw]
    ins += tail
    specs += [full(a) for a in tail]
    return pl.pallas_call(
        kern,
        grid=(nch,),
        in_specs=specs,
        out_specs=blk(SSD_W),
        out_shape=jax.ShapeDtypeStruct((nb, rb, SSD_W), F32 if direction == 0 else BF16),
        scratch_shapes=[pltpu.VMEM((nb, SSD_GROUPS, SSD_STATE, 256), F32)],
        compiler_params=_scan_params(),
        name="ssd_fwd" if direction == 0 else "ssd_bwd",
    )(*ins)


def _ret_log_gamma_row(width):
    lane = lax.broadcasted_iota(jnp.int32, (1, width), 1)
    row = jnp.zeros((1, width), F32)
    for h in range(RET_HEADS):
        lg = math.log1p(-(2.0 ** (-5.0 - h)))
        row = jnp.where(lane // RET_HEAD_DIM == h, lg, row)
    return row


def _rope(t, cos, sin_signed):
    lane = lax.broadcasted_iota(jnp.int32, t.shape, 1)
    half = RET_HEAD_DIM // 2
    swapped = jnp.where(lane % RET_HEAD_DIM < half,
                        pltpu.roll(t, LANES - half, 1), pltpu.roll(t, half, 1))
    return t * cos + swapped * sin_signed


def _ret_kernel(*refs, direction, nb, c, nck):
    if direction == 0:
        (p_ref, cos_ref, sin_ref, avg_ref, hm_ref, o_ref, st_ref) = refs
    else:
        (p_ref, cos_ref, sin_ref, of_ref, avg_ref, hm_ref, nw_ref, o_ref, st_ref) = refs
    w = RET_W

    @pl.when(pl.program_id(0) == 0)
    def _():
        st_ref[...] = jnp.zeros_like(st_ref)

    lg_row = _ret_log_gamma_row(w)
    pos = lax.broadcasted_iota(jnp.int32, (c, 1), 0)
    cnt_in = (pos + 1 if direction == 0 else c - pos).astype(F32)
    cnt_st = (c - 1 - pos if direction == 0 else pos).astype(F32)
    e_in = jnp.exp(cnt_in * lg_row)
    e_st = jnp.exp(cnt_st * lg_row)
    e_all = jnp.exp(float(c) * lg_row)
    ii = lax.broadcasted_iota(jnp.int32, (c, c), 0)
    jj = lax.broadcasted_iota(jnp.int32, (c, c), 1)
    dist = (ii - jj) if direction == 0 else (jj - ii)
    dec_parts = []
    for h in range(RET_HEADS):
        lg = math.log1p(-(2.0 ** (-5.0 - h)))
        dec_parts.append(jnp.where(dist >= 0, jnp.exp(dist.astype(F32) * lg), 0.0))
    dec = jnp.concatenate(dec_parts, axis=1)
    bd_mask = avg_ref[...].astype(F32)

    order = list(range(nck)) if direction == 0 else list(range(nck - 1, -1, -1))
    rows = {ci: slice(ci * c, (ci + 1) * c) for ci in order}
    units = [(ci, b) for ci in order for b in range(nb)]
    hm_b = hm_ref[...].astype(BF16)
    qbs, kbs, vbs, kes = {}, {}, {}, {}
    for ci, b in units:
        cos = cos_ref[rows[ci], :]
        sin = sin_ref[rows[ci], :]
        q_halves, k_halves = [], []
        for hf in range(w // LANES):
            qh = p_ref[b, rows[ci], hf * LANES:(hf + 1) * LANES] * (RET_HEAD_DIM ** -0.5)
            kh = p_ref[b, rows[ci], w + hf * LANES:w + (hf + 1) * LANES]
            q_halves.append(_rope(qh, cos, sin))
            k_halves.append(_rope(kh, cos, sin))
        k = jnp.concatenate(k_halves, axis=1)
        qbs[ci, b] = jnp.concatenate(q_halves, axis=1).astype(BF16)
        kbs[ci, b] = k.astype(BF16)
        kes[ci, b] = (k * e_st).astype(BF16)
        vbs[ci, b] = p_ref[b, rows[ci], 2 * w:3 * w].astype(BF16)
    k_bd = {u: jnp.concatenate([kbs[u] * hm_b[h:h + 1, :] for h in range(RET_HEADS)], axis=0) for u in units}
    v_bd = {u: jnp.concatenate([vbs[u] * hm_b[h:h + 1, :] for h in range(RET_HEADS)], axis=0) for u in units}
    scores = {u: (_dot_nt(qbs[u], k_bd[u]) * dec).astype(BF16) for u in units}
    intra = {u: _dot(scores[u], v_bd[u]) for u in units}
    sts = [st_ref[b] for b in range(nb)]
    outs = {}
    for ci in order:
        for b in range(nb):
            outs[ci, b] = intra[ci, b] + e_in * _dot(qbs[ci, b], sts[b].astype(BF16))
        for b in range(nb):
            sts[b] = sts[b] * e_all + _dot_tn(kes[ci, b], vbs[ci, b]) * bd_mask
    for b in range(nb):
        st_ref[b] = sts[b]
    if direction == 0:
        for ci, b in units:
            o_ref[b, rows[ci], :] = outs[ci, b]
    else:
        tots = {(ci, b): of_ref[b, rows[ci], :] + outs[ci, b] for ci, b in units}
        xcs = {u: tots[u] - _dot_sel_lhs(tots[u], avg_ref[...], 2) * (1.0 / RET_HEAD_DIM) for u in units}
        vrs = {u: _dot_sel_lhs(xcs[u] * xcs[u], avg_ref[...], 2) * (1.0 / RET_HEAD_DIM) for u in units}
        for ci, b in units:
            y = xcs[ci, b] * lax.rsqrt(vrs[ci, b] + RMS_EPS) * nw_ref[...]
            o_ref[b, rows[ci], :] = (y * _silu(p_ref[b, rows[ci], 3 * w:4 * w])).astype(BF16)


def _ret_call(direction, p_ret, cos_t, sin_t, o_f, nw, consts, *, nctx):
    nb, rb, _ = p_ret.shape
    c = RET_CHUNK
    step = SCAN_STEP_ROWS
    nch, ncc = rb // step, nctx // step
    kern = functools.partial(_ret_kernel, direction=direction, nb=nb, c=c, nck=step // c)
    chunk = lambda n: (0, _chunk_of(direction, n, ncc, nch), 0)
    tchunk = lambda n: (_chunk_of(direction, n, ncc, nch), 0)
    const2 = lambda n: (0, 0)
    blk = lambda w: pl.BlockSpec((nb, step, w), chunk)
    full = lambda a: pl.BlockSpec(a.shape, const2)
    ins = [p_ret, cos_t, sin_t]
    specs = [blk(4 * RET_W), pl.BlockSpec((step, LANES), tchunk), pl.BlockSpec((step, LANES), tchunk)]
    if direction == 1:
        ins.append(o_f)
        specs.append(blk(RET_W))
    tail = [consts["avg256"], consts["hmask"]]
    if direction == 1:
        tail.append(nw)
    ins += tail
    specs += [full(a) for a in tail]
    return pl.pallas_call(
        kern,
        grid=(nch,),
        in_specs=specs,
        out_specs=blk(RET_W),
        out_shape=jax.ShapeDtypeStruct((nb, rb, RET_W), F32 if direction == 0 else BF16),
        scratch_shapes=[pltpu.VMEM((nb, RET_W, RET_W), F32)],
        compiler_params=_scan_params(),
        name="ret_fwd" if direction == 0 else "ret_bwd",
    )(*ins)


def _mixffn_kernel(h0_ref, g0_ref, s0_ref, r0_ref, hn_ref, gn_ref, sn_ref, rn_ref, mod_ref,
                   wo_ref, nmix_ref, npre_ref, w13_ref, w2_ref, npost_ref, o_ref,
                   h1a_ref, h1b_ref, hba_ref, hbb_ref, acc_ref, *, tm, spb, ncs, nb):
    i = pl.program_id(0)
    n = pl.num_programs(0)
    d = D_MODEL
    nsl = tm // SLAB
    nck = FFN_HIDDEN // FFN_CHUNK

    def prologue(h_ref, g_ref, s_ref, r_ref, tile, h1_ref, hb_ref):
        mixed = jnp.concatenate([g_ref[...], s_ref[...], r_ref[...]], axis=1)
        m = _dot(mixed, wo_ref[...])
        for k in range(nsl):
            mod = mod_ref[_mod_row(tile * nsl + k, spb, ncs, nb)]
            sl = slice(k * SLAB, (k + 1) * SLAB)
            h1 = h_ref[sl, :] + mod[:, 2 * d:3 * d] * _rms(m[sl, :], nmix_ref[...])
            h1_ref[sl, :] = h1
            y = _rms(h1, npre_ref[...]) * (1.0 + mod[:, 4 * d:5 * d]) + mod[:, 3 * d:4 * d]
            hb_ref[sl, :] = y.astype(BF16)

    def ffn(h1_ref, hb_ref):
        hb = hb_ref[...]
        for cidx in range(nck):
            lo = cidx * FFN_CHUNK
            gate = _dot(hb, w13_ref[:, lo:lo + FFN_CHUNK])
            up = _dot(hb, w13_ref[:, FFN_HIDDEN + lo:FFN_HIDDEN + lo + FFN_CHUNK])
            part = _dot((_silu(gate) * up).astype(BF16), w2_ref[lo:lo + FFN_CHUNK, :])
            if cidx == 0:
                acc_ref[...] = part
            else:
                acc_ref[...] += part
        for k in range(nsl):
            mod = mod_ref[_mod_row(i * nsl + k, spb, ncs, nb)]
            sl = slice(k * SLAB, (k + 1) * SLAB)
            o_ref[sl, :] = h1_ref[sl, :] + mod[:, 5 * d:6 * d] * _rms(acc_ref[sl, :], npost_ref[...])

    @pl.when(i == 0)
    def _():
        prologue(h0_ref, g0_ref, s0_ref, r0_ref, 0, h1a_ref, hba_ref)

    nxt = jnp.minimum(i + 1, n - 1)

    @pl.when(i % 2 == 0)
    def _():
        prologue(hn_ref, gn_ref, sn_ref, rn_ref, nxt, h1b_ref, hbb_ref)
        ffn(h1a_ref, hba_ref)

    @pl.when(i % 2 == 1)
    def _():
        prologue(hn_ref, gn_ref, sn_ref, rn_ref, nxt, h1a_ref, hba_ref)
        ffn(h1b_ref, hbb_ref)


def _mixffn_call(h, mod, gla, ssd, ret, wo, nmix, npre, w13, w2, npost, *, layer, spb, ncs, nb):
    r, d = h.shape
    tm = TM_DENSE
    nt = r // tm
    kern = functools.partial(_mixffn_kernel, tm=tm, spb=spb, ncs=ncs, nb=nb)
    row = lambda i: (i, 0)
    first = lambda i: (0, 0)
    ahead = lambda i: (jnp.minimum(i + 1, nt - 1), 0)
    const = lambda i: (0, 0)

    def resident(w):
        zeros = (0,) * (w.ndim - 1)
        return pl.BlockSpec((None,) + w.shape[1:], lambda i: (layer,) + zeros, pipeline_mode=pl.Buffered(1))

    tiles = lambda imap: [pl.BlockSpec((tm, d), imap), pl.BlockSpec((tm, GLA_W), imap),
                          pl.BlockSpec((tm, SSD_W), imap), pl.BlockSpec((tm, RET_W), imap)]
    return pl.pallas_call(
        kern,
        grid=(nt,),
        in_specs=tiles(first) + tiles(ahead) + [
            pl.BlockSpec(mod.shape, lambda i: (0, 0, 0)),
            resident(wo),
            pl.BlockSpec((1, d), const),
            pl.BlockSpec((1, d), const),
            resident(w13),
            resident(w2),
            pl.BlockSpec((1, d), const),
        ],
        out_specs=pl.BlockSpec((tm, d), row),
        out_shape=jax.ShapeDtypeStruct((r, d), F32),
        scratch_shapes=[pltpu.VMEM((tm, d), F32), pltpu.VMEM((tm, d), F32),
                        pltpu.VMEM((tm, d), BF16), pltpu.VMEM((tm, d), BF16), pltpu.VMEM((tm, d), F32)],
        compiler_params=pltpu.CompilerParams(
            dimension_semantics=("arbitrary",), vmem_limit_bytes=VMEM_LIMIT),
        name="mix_ffn",
    )(h, gla, ssd, ret, h, gla, ssd, ret, mod, wo, nmix, npre, w13, w2, npost)


def _tri_pair(c):
    i = jnp.arange(c)[:, None]
    j = jnp.arange(c)[None, :]
    return jnp.stack([(j <= i), (j >= i)]).astype(BF16)


def _gla_cum_pair(cg, sb):
    i = jnp.arange(cg)[:, None]
    j = jnp.arange(cg)[None, :]
    same = (i // sb) == (j // sb)
    fwd = jnp.concatenate([same & (j <= i), same], axis=0)
    bwd = jnp.concatenate([same & (j >= i), same], axis=0)
    return jnp.stack([fwd, bwd]).astype(BF16)


def _constants():
    lane256 = jnp.arange(256)
    consts = {
        "gla_cum": _gla_cum_pair(GLA_BLOCK, GLA_SUB),
        "tri_ssd": _tri_pair(SSD_CHUNK),
        "avg256": (lane256[:, None] // 64 == lane256[None, :] // 64).astype(BF16),
        "gla_e2": (jnp.arange(GLA_KDIM)[:, None] // GLA_HEAD_K == lane256[None, :] // GLA_HEAD_V).astype(BF16),
        "gla_bdm": (lane256[:, None] // GLA_HEAD_V == jnp.arange(GLA_KDIM)[None, :] // GLA_HEAD_K).astype(F32),
        "ssd_x": (jnp.arange(LANES)[:, None] - SMALL_DT_LANE
                  == jnp.arange(SSD_W)[None, :] // SSD_HEAD_DIM).astype(BF16),
        "hmask": (jnp.arange(8)[:, None] == lane256[None, :] // 64).astype(F32),
    }
    return consts


def _rope_tables(t_lat, nctx):
    rows = t_lat // GRID_W
    row = jnp.repeat(jnp.arange(rows), GRID_W).astype(F32)
    col = jnp.tile(jnp.arange(GRID_W), rows).astype(F32)
    n_freq = RET_HEAD_DIM // 4
    inv_freq = ROPE_BASE ** (-jnp.arange(n_freq, dtype=F32) / n_freq)
    ang = jnp.concatenate([row[:, None] * inv_freq, col[:, None] * inv_freq], axis=-1)
    cos = jnp.cos(ang)
    sin = jnp.sin(ang)
    cos_t = jnp.concatenate([cos, cos, cos, cos], axis=-1)
    sin_t = jnp.concatenate([-sin, sin, -sin, sin], axis=-1)
    cos_t = jnp.concatenate([jnp.ones((nctx, LANES), F32), cos_t], axis=0)
    sin_t = jnp.concatenate([jnp.zeros((nctx, LANES), F32), sin_t], axis=0)
    return cos_t, sin_t


def kernel(x, c, ctx, c_ctx, ada_w, ada_b, norm_mix_pre, norm_mix_post, norm_ffn_pre, norm_ffn_post,
           w_in, w_out, gla_gate_up, gla_gate_b, gla_norm, ssd_conv_w, ssd_conv_b, ssd_dt_bias,
           ssd_a_log, ssd_d, ssd_norm, ret_norm, ffn_w13, ffn_w2):
    nb, t_lat, d = x.shape
    nctx = ctx.shape[1]
    depth = ada_w.shape[0]
    rb = nctx + t_lat
    r = nb * rb
    spb = rb // SLAB
    ncs = nctx // SLAB
    assert d == D_MODEL and nb + 1 <= 8
    assert nctx % SLAB == 0 and t_lat % SLAB == 0 and r % TM_DENSE == 0
    assert all(n % s == 0 for n in (nctx, t_lat) for s in (SCAN_STEP_ROWS, GLA_BLOCK))

    consts = _constants()
    cos_t, sin_t = _rope_tables(t_lat, nctx)

    cvec = jnp.zeros((8, d), F32).at[:nb].set(c).at[nb].set(c_ctx)
    mods = _ada_call(cvec, ada_w, ada_b)

    s1, s2 = GLA_COLS, GLA_COLS + SSD_COLS
    lr0 = 2 * GLA_KDIM + 2 * GLA_W
    dt0 = s1 + SSD_W + SSD_CONV_CH
    wm = jnp.concatenate([w_in[:, :, 0:lr0], w_in[:, :, s1:dt0], w_in[:, :, s2:]], axis=-1).astype(BF16)
    zpad = jnp.zeros((depth, d, LANES - GLA_RANK - SSD_HEADS), F32)
    ws = jnp.concatenate(
        [w_in[:, :, lr0:lr0 + GLA_RANK], w_in[:, :, dt0:dt0 + SSD_HEADS], zpad,
         w_in[:, :, lr0 + GLA_RANK:lr0 + 2 * GLA_RANK], w_in[:, :, dt0 + SSD_HEADS:dt0 + 2 * SSD_HEADS], zpad],
        axis=-1).astype(BF16)
    wo = w_out.astype(BF16)
    w13 = ffn_w13.astype(BF16)
    w2 = ffn_w2.astype(BF16)

    wg = jnp.zeros((depth, 2, LANES, GLA_KDIM), F32).at[:, :, :GLA_RANK, :].set(gla_gate_up)
    gb = gla_gate_b.reshape(depth, 2, 1, GLA_KDIM)
    lane_dt = slice(SMALL_DT_LANE, SMALL_DT_LANE + SSD_HEADS)
    dtb = jnp.zeros((depth, 2, 1, LANES), F32).at[:, :, 0, lane_dt].set(ssd_dt_bias)
    alog = jnp.zeros((depth, 2, 1, LANES), F32).at[:, :, 0, lane_dt].set(ssd_a_log)
    dsk = jnp.repeat(ssd_d, SSD_HEAD_DIM, axis=-1).reshape(depth, 1, SSD_W)
    conv_w = jnp.zeros((depth, 8, SSD_CONV_CH), F32).at[:, :SSD_CONV_W].set(ssd_conv_w)

    h = jnp.concatenate([ctx, x], axis=1).reshape(r, d)
    meta = dict(spb=spb, ncs=ncs, nb=nb)
    row1 = lambda a: a.reshape(1, -1)
    for l in range(depth):
        mod = mods[l].reshape(8, 1, 6 * d)
        p_gla, p_z, xconv, p_ret, small_f, small_b = _inproj_call(
            h, mod, row1(norm_mix_pre[l]), wm, ws, conv_w, row1(ssd_conv_b[l]), layer=l, **meta)
        small = (small_f.reshape(nb, rb, LANES), small_b.reshape(nb, rb, LANES))
        xconv = xconv.reshape(nb, rb, SSD_CONV_CH)
        p_gla = p_gla.reshape(nb, rb, 768)
        p_z = p_z.reshape(nb, rb, SSD_W)
        p_ret = p_ret.reshape(nb, rb, 4 * RET_W)
        gla_f = _gla_call(0, p_gla, small[0], None, wg[l, 0], gb[l, 0], None, consts, nctx=nctx)
        gla = _gla_call(1, p_gla, small[1], gla_f, wg[l, 1], gb[l, 1], row1(gla_norm[l]), consts, nctx=nctx)
        ssd_f = _ssd_call(0, xconv, small[0], None, None, dtb[l, 0], alog[l, 0], None, None, consts, nctx=nctx)
        ssd = _ssd_call(1, xconv, small[1], p_z, ssd_f, dtb[l, 1], alog[l, 1], dsk[l], row1(ssd_norm[l]),
                        consts, nctx=nctx)
        ret_f = _ret_call(0, p_ret, cos_t, sin_t, None, None, consts, nctx=nctx)
        ret = _ret_call(1, p_ret, cos_t, sin_t, ret_f, row1(ret_norm[l]), consts, nctx=nctx)
        h = _mixffn_call(h, mod, gla.reshape(r, GLA_W), ssd.reshape(r, SSD_W), ret.reshape(r, RET_W),
                         wo, row1(norm_mix_post[l]), row1(norm_ffn_pre[l]), w13, w2, row1(norm_ffn_post[l]),
                         layer=l, **meta)
    return h.reshape(nb, rb, d)[:, nctx:, :]
```
